```python
import math, functools
import jax, jax.numpy as jnp
from jax import lax
import numpy as np

D_MODEL = 1024
BATCH = 16
SEQ = 2048
DEPTH = 1
DEC_BATCH = 128
DEC_SEQ = 8
PAST_LEN = 8192
PAGE_SIZE = 128

N_HEADS = 16
HEAD_DIM = 64
D_ATTN = N_HEADS * HEAD_DIM
POOL_WINDOWS = (2, 4, 8, 16)
N_POOL_GROUPS = len(POOL_WINDOWS)
D_POOL = D_MODEL // 2
POOL_GROUP_IN = D_POOL // N_POOL_GROUPS
POOL_GROUP_OUT = D_MODEL // N_POOL_GROUPS
POOL_STATE = max(POOL_WINDOWS) - 1
D_FF = 4 * D_MODEL
Q_BLOCK = 128
RMS_EPS = 1e-6
FORGET_BIAS_INIT = 4.0
SPLIT_SIZES = (D_ATTN, D_ATTN, D_ATTN, N_HEADS, D_POOL, D_MODEL, D_MODEL)
SPLIT_POINTS = tuple(int(s) for s in np.cumsum(SPLIT_SIZES)[:-1])
D_IN_TOTAL = sum(SPLIT_SIZES)

kernel_name = 'fox_pool_hybrid_decode_step'


def _rmsnorm(x, g):
    xf = x.astype(jnp.float32)
    y = xf * lax.rsqrt(jnp.mean(xf * xf, axis=-1, keepdims=True) + RMS_EPS)
    return (y * g.astype(jnp.float32)).astype(x.dtype)


def _mixer_inputs(h, w_in, b_forget):
    B, S = h.shape[:2]
    proj = h @ w_in
    q, k, v, f, u, ga, gb = jnp.split(proj, SPLIT_POINTS, axis=-1)
    q = q.reshape(B, S, N_HEADS, HEAD_DIM)
    k = k.reshape(B, S, N_HEADS, HEAD_DIM)
    v = v.reshape(B, S, N_HEADS, HEAD_DIM)
    logf = jax.nn.log_sigmoid((f + b_forget).astype(jnp.float32))
    return q, k, v, logf, u, ga, gb


def _fox_prompt(q, k, v, logf):
    B, S, H, Dh = q.shape
    nb = S // Q_BLOCK
    scale = 1.0 / math.sqrt(Dh)
    Fk = jnp.transpose(lax.cumsum(logf, axis=1), (0, 2, 1))
    kf = k.astype(jnp.float32)
    vf = v.astype(jnp.float32)
    qb = q.astype(jnp.float32).reshape(B, nb, Q_BLOCK, H, Dh).transpose(1, 0, 2, 3, 4)
    Fqb = Fk.reshape(B, H, nb, Q_BLOCK).transpose(2, 0, 1, 3)
    posb = jnp.arange(S).reshape(nb, Q_BLOCK)
    key_pos = jnp.arange(S)

    def block(args):
        qi, Fqi, pi = args
        s = jnp.einsum('bqhd,bkhd->bhqk', qi, kf) * scale + Fqi[..., None] - Fk[:, :, None, :]
        s = jnp.where(pi[:, None] >= key_pos[None, :], s, -jnp.inf)
        p = jax.nn.softmax(s, axis=-1)
        return jnp.einsum('bhqk,bkhd->bqhd', p, vf)

    o = lax.map(block, (qb, Fqb, posb))
    return o.transpose(1, 0, 2, 3, 4).reshape(B, S, H * Dh)


def _fox_sample(q, k, v, logf, cache_k, cache_v, cache_logf, page_table):
    DB, T, H, Dh = q.shape
    n_pages = page_table.shape[1]
    scale = 1.0 / math.sqrt(Dh)
    qf = q.astype(jnp.float32)
    kf = k.astype(jnp.float32)
    vf = v.astype(jnp.float32)
    lf_past = cache_logf[page_table].astype(jnp.float32).reshape(DB, n_pages * PAGE_SIZE, H)
    R = lax.cumsum(lf_past, axis=1, reverse=True) - lf_past
    R_pages = R.reshape(DB, n_pages, PAGE_SIZE, H).transpose(1, 0, 3, 2)
    Fq = jnp.transpose(lax.cumsum(logf, axis=1), (0, 2, 1))
    s = jnp.einsum('bqhd,bkhd->bhqk', qf, kf) * scale + Fq[..., :, None] - Fq[..., None, :]
    causal = jnp.arange(T)[:, None] >= jnp.arange(T)[None, :]
    s = jnp.where(causal, s, -jnp.inf)
    m = jnp.max(s, axis=-1)
    p = jnp.exp(s - m[..., None])
    l = jnp.sum(p, axis=-1)
    acc = jnp.einsum('bhqk,bkhd->bhqd', p, vf)

    def step(carry, xs):
        m, l, acc = carry
        phys, Rp = xs
        kp = cache_k[phys].astype(jnp.float32)
        vp = cache_v[phys].astype(jnp.float32)
        sp = jnp.einsum('bqhd,bkhd->bhqk', qf, kp) * scale + Fq[..., None] + Rp[:, :, None, :]
        m_new = jnp.maximum(m, jnp.max(sp, axis=-1))
        corr = jnp.exp(m - m_new)
        pp = jnp.exp(sp - m_new[..., None])
        l_new = l * corr + jnp.sum(pp, axis=-1)
        acc_new = acc * corr[..., None] + jnp.einsum('bhqk,bkhd->bhqd', pp, vp)
        return (m_new, l_new, acc_new), None

    (m, l, acc), _ = lax.scan(step, (m, l, acc), (page_table.T, R_pages))
    o = acc / l[..., None]
    return o.transpose(0, 2, 1, 3).reshape(DB, T, H * Dh)


def _pool_branch(u_ext, pos, w_pool, pool_scale):
    B = u_ext.shape[0]
    L = pos.shape[0]
    P = u_ext.shape[1] - L
    uf = u_ext.astype(jnp.float32)
    c = jnp.concatenate([jnp.zeros((B, 1, D_POOL), jnp.float32), lax.cumsum(uf, axis=1)], axis=1)
    cur = uf[:, P:]
    outs = []
    for g, w in enumerate(POOL_WINDOWS):
        lo, hi = g * POOL_GROUP_IN, (g + 1) * POOL_GROUP_IN
        win = c[:, P + 1:, lo:hi] - c[:, P + 1 - w:P + 1 - w + L, lo:hi]
        cnt = jnp.minimum(pos + 1, w).astype(jnp.float32)[None, :, None]
        outs.append(win / cnt - cur[..., lo:hi])
    pooled = jnp.stack(outs, axis=2)
    y = jnp.einsum('bsgc,gcd->bsgd', pooled, w_pool.astype(jnp.float32)).reshape(B, L, D_MODEL)
    return (y * pool_scale.astype(jnp.float32)).astype(u_ext.dtype)


def _layer(x, u_prev, pos, attend, g_pre, g_post, w_in, b_forget, w_pool, pool_scale, w_out,
           g_mlp_pre, g_mlp_post, w_up, w_down):
    h = _rmsnorm(x, g_pre)
    q, k, v, logf, u, ga, gb = _mixer_inputs(h, w_in, b_forget)
    a = attend(q, k, v, logf).astype(x.dtype)
    u_ext = jnp.concatenate([u_prev.astype(u.dtype), u], axis=1)
    pooled = _pool_branch(u_ext, pos, w_pool, pool_scale)
    z = jax.nn.sigmoid(ga) * a + jax.nn.sigmoid(gb) * pooled
    x = x + _rmsnorm(z @ w_out, g_post)
    h2 = _rmsnorm(x, g_mlp_pre)
    x = x + _rmsnorm(jnp.square(jax.nn.relu(h2 @ w_up)) @ w_down, g_mlp_post)
    return x, k, v, logf, u_ext[:, -POOL_STATE:]


def setup_inputs(seed: int = 0) -> dict:
    key = jax.random.key(seed)
    ks = jax.random.split(key, 24)
    f32 = jnp.float32
    n_pages = PAST_LEN // PAGE_SIZE
    n_used = DEC_BATCH * n_pages
    n_phys = n_used + n_used // 4
    x_prompt = jax.random.normal(ks[0], (BATCH, SEQ, D_MODEL), f32)
    x_sample = jax.random.normal(ks[1], (DEC_BATCH, DEC_SEQ, D_MODEL), f32)
    cache_k = jax.random.normal(ks[2], (DEPTH, n_phys, PAGE_SIZE, N_HEADS, HEAD_DIM), f32)
    cache_v = jax.random.normal(ks[3], (DEPTH, n_phys, PAGE_SIZE, N_HEADS, HEAD_DIM), f32)
    cache_logf = jax.nn.log_sigmoid(6.0 + 0.5 * jax.random.normal(ks[4], (DEPTH, n_phys, PAGE_SIZE, N_HEADS), f32))
    state_pool = jax.random.normal(ks[5], (DEPTH, DEC_BATCH, POOL_STATE, D_POOL), f32)
    page_table = jax.random.permutation(ks[6], n_phys)[:n_used].reshape(DEC_BATCH, n_pages).astype(jnp.int32)
    norm_attn_pre = 1.0 + 0.1 * jax.random.normal(ks[7], (DEPTH, D_MODEL), f32)
    norm_attn_post = 1.0 + 0.1 * jax.random.normal(ks[8], (DEPTH, D_MODEL), f32)
    w_in = jax.random.normal(ks[9], (DEPTH, D_MODEL, D_IN_TOTAL), f32) * D_MODEL ** -0.5
    b_forget = FORGET_BIAS_INIT + 0.5 * jax.random.normal(ks[10], (DEPTH, N_HEADS), f32)
    w_pool = jax.random.normal(ks[11], (DEPTH, N_POOL_GROUPS, POOL_GROUP_IN, POOL_GROUP_OUT), f32) * POOL_GROUP_IN ** -0.5
    pool_scale = 1.0 + 0.1 * jax.random.normal(ks[12], (DEPTH, D_MODEL), f32)
    w_out = jax.random.normal(ks[13], (DEPTH, D_MODEL, D_MODEL), f32) * D_MODEL ** -0.5
    norm_mlp_pre = 1.0 + 0.1 * jax.random.normal(ks[14], (DEPTH, D_MODEL), f32)
    norm_mlp_post = 1.0 + 0.1 * jax.random.normal(ks[15], (DEPTH, D_MODEL), f32)
    w_up = jax.random.normal(ks[16], (DEPTH, D_MODEL, D_FF), f32) * D_MODEL ** -0.5
    w_down = jax.random.normal(ks[17], (DEPTH, D_FF, D_MODEL), f32) * D_FF ** -0.5
    return {'x_prompt': x_prompt, 'x_sample': x_sample, 'cache_k': cache_k, 'cache_v': cache_v,
            'cache_logf': cache_logf, 'state_pool': state_pool, 'page_table': page_table,
            'norm_attn_pre': norm_attn_pre, 'norm_attn_post': norm_attn_post, 'w_in': w_in,
            'b_forget': b_forget, 'w_pool': w_pool, 'pool_scale': pool_scale, 'w_out': w_out,
            'norm_mlp_pre': norm_mlp_pre, 'norm_mlp_post': norm_mlp_post, 'w_up': w_up, 'w_down': w_down}


def reference(x_prompt, x_sample, cache_k, cache_v, cache_logf, state_pool, page_table,
              norm_attn_pre, norm_attn_post, w_in, b_forget, w_pool, pool_scale, w_out,
              norm_mlp_pre, norm_mlp_post, w_up, w_down):
    B, S = x_prompt.shape[:2]
    T = x_sample.shape[1]
    past_len = page_table.shape[1] * PAGE_SIZE
    pos_prompt = jnp.arange(S)
    pos_sample = past_len + jnp.arange(T)
    xp, xs = x_prompt, x_sample
    kp_l, vp_l, lfp_l, up_l = [], [], [], []
    ks_l, vs_l, lfs_l, us_l = [], [], [], []
    for d in range(DEPTH):
        wts = (norm_attn_pre[d], norm_attn_post[d], w_in[d], b_forget[d], w_pool[d], pool_scale[d],
               w_out[d], norm_mlp_pre[d], norm_mlp_post[d], w_up[d], w_down[d])
        zeros_prev = jnp.zeros((B, POOL_STATE, D_POOL), x_prompt.dtype)
        xp, kp, vp, lfp, up = _layer(xp, zeros_prev, pos_prompt, _fox_prompt, *wts)
        ck, cv, cl = cache_k[d], cache_v[d], cache_logf[d]
        attend_s = lambda q, k, v, lf: _fox_sample(q, k, v, lf, ck, cv, cl, page_table)
        xs, ksn, vsn, lfs, us = _layer(xs, state_pool[d], pos_sample, attend_s, *wts)
        kp_l.append(kp); vp_l.append(vp); lfp_l.append(lfp); up_l.append(up)
        ks_l.append(ksn); vs_l.append(vsn); lfs_l.append(lfs); us_l.append(us)
    k_prompt = jnp.stack(kp_l)
    v_prompt = jnp.stack(vp_l)
    logf_prompt = jnp.stack(lfp_l)
    pool_prompt = jnp.stack(up_l)
    k_sample = jnp.stack(ks_l)
    v_sample = jnp.stack(vs_l)
    logf_sample = jnp.stack(lfs_l)
    pool_sample = jnp.stack(us_l)
    return (xp, xs, k_prompt, v_prompt, logf_prompt, pool_prompt, k_sample, v_sample, logf_sample, pool_sample)
```

```python
import functools
import math

import jax
import jax.numpy as jnp
from jax import lax
from jax.experimental import pallas as pl
from jax.experimental.pallas import tpu as pltpu

N_HEADS = 16
HEAD_DIM = 64
POOL_WINDOWS = (2, 4, 8, 16)
POOL_STATE = max(POOL_WINDOWS) - 1
POOL_GROUP_IN = 128
POOL_GROUP_OUT = 256
RMS_EPS = 1e-6
PAGE_SIZE = 128
SCALE = 1.0 / math.sqrt(HEAD_DIM)

LANES = 128
SUBLANES = 8
HEADS_PER_BLOCK = LANES // HEAD_DIM
F_COPIES = LANES // N_HEADS
VMEM_LIMIT = 56 * 1024 * 1024

F32 = jnp.float32
BF16 = jnp.bfloat16
NEG_INF = float("-inf")

_dot = functools.partial(jnp.dot, preferred_element_type=F32)


def _dot_nt(a, b):
    return lax.dot_general(a, b, (((1,), (1,)), ((), ())), preferred_element_type=F32)


def _rms(x, g):
    return x * lax.rsqrt(jnp.mean(x * x, axis=-1, keepdims=True) + RMS_EPS) * g


def _sigmoid(x):
    return 1.0 / (1.0 + jnp.exp(-x))


def _log_sigmoid(x):
    return jnp.minimum(x, 0.0) - jnp.log1p(jnp.exp(-jnp.abs(x)))


def _split3(f):
    hi = f.astype(BF16).astype(F32)
    r = f - hi
    mid = r.astype(BF16).astype(F32)
    lo = (r - mid).astype(BF16).astype(F32)
    return hi, mid, lo


def _proj_kernel(x_ref, g_ref, wqkv_ref, wf_ref, bf_ref, wu_ref, wg_ref,
                 q_ref, k_ref, v_ref, lf_ref, u_ref, sga_ref, sgb_ref):
    d = x_ref.shape[1]
    h = _rms(x_ref[...], g_ref[...]).astype(BF16)
    q_ref[...] = (_dot(h, wqkv_ref[:, 0:d]) * SCALE).astype(BF16)
    k_ref[...] = _dot(h, wqkv_ref[:, d:2 * d])
    v_ref[...] = _dot(h, wqkv_ref[:, 2 * d:3 * d])
    lf_ref[...] = _log_sigmoid(_dot(h, wf_ref[...]) + bf_ref[...])
    u_ref[...] = _dot(h, wu_ref[...])
    sga_ref[...] = _sigmoid(_dot(h, wg_ref[:, 0:d])).astype(BF16)
    sgb_ref[...] = _sigmoid(_dot(h, wg_ref[:, d:2 * d])).astype(BF16)


def _const_spec(shape):
    return pl.BlockSpec(shape, lambda *_: (0,) * len(shape))


def _proj(x, g, wqkv, wf, bf, wu, wg, *, tm):
    n, d = x.shape
    d_pool = wu.shape[1]
    row = lambda w: pl.BlockSpec((tm, w), lambda i: (i, 0))
    return pl.pallas_call(
        _proj_kernel,
        grid=(n // tm,),
        in_specs=[row(d), _const_spec(g.shape), _const_spec(wqkv.shape), _const_spec(wf.shape),
                  _const_spec(bf.shape), _const_spec(wu.shape), _const_spec(wg.shape)],
        out_specs=[row(d), row(d), row(d), row(LANES), row(d_pool), row(d), row(d)],
        out_shape=[jax.ShapeDtypeStruct((n, d), BF16), jax.ShapeDtypeStruct((n, d), F32),
                   jax.ShapeDtypeStruct((n, d), F32), jax.ShapeDtypeStruct((n, LANES), F32),
                   jax.ShapeDtypeStruct((n, d_pool), F32), jax.ShapeDtypeStruct((n, d), BF16),
                   jax.ShapeDtypeStruct((n, d), BF16)],
        compiler_params=pltpu.CompilerParams(dimension_semantics=("parallel",),
                                             vmem_limit_bytes=VMEM_LIMIT),
        name="proj",
    )(x, g, wqkv, wf, bf, wu, wg)


def _cumsum_kernel(lf_ref, f_ref, buf_ref):
    s = lf_ref.shape[1]
    pad = buf_ref.shape[0] - s
    buf_ref[0:pad, :] = jnp.zeros((pad, LANES), F32)
    buf_ref[pad:pad + s, :] = lf_ref[0]
    d = 1
    while d < s:
        y = buf_ref[pad:pad + s, :] + buf_ref[pad - d:pad - d + s, :]
        buf_ref[pad:pad + s, :] = y
        d *= 2
    f_ref[0] = buf_ref[pad:pad + s, :]


def _cumsum(lf):
    b, s, _ = lf.shape
    spec = pl.BlockSpec((1, s, LANES), lambda i: (i, 0, 0))
    return pl.pallas_call(
        _cumsum_kernel,
        grid=(b,),
        in_specs=[spec],
        out_specs=spec,
        out_shape=jax.ShapeDtypeStruct(lf.shape, F32),
        scratch_shapes=[pltpu.VMEM((s + s // 2, LANES), F32)],
        compiler_params=pltpu.CompilerParams(dimension_semantics=("parallel",)),
        name="cumsum",
    )(lf)


def _aug_q(f, base, lane):
    hi, mid, lo = _split3(f)
    rel = lane - base
    one = jnp.where((rel >= 3) & (rel < 6), 1.0, 0.0)
    return jnp.where(rel == 0, hi, jnp.where(rel == 1, mid, jnp.where(rel == 2, lo, one))).astype(BF16)


def _aug_k(f, base, lane):
    hi, mid, lo = _split3(f)
    rel = lane - base
    one = jnp.where((rel >= 0) & (rel < 3), 1.0, 0.0)
    return jnp.where(rel == 3, -hi, jnp.where(rel == 4, -mid, jnp.where(rel == 5, -lo, one))).astype(BF16)


def _attn_kernel(q_ref, k_ref, v_ref, fq_ref, fk_ref, o_ref, kcat_ref, vcat_ref, *, tq, tk):
    hp = pl.program_id(1)
    i = pl.program_id(2)
    s_len = k_ref.shape[0]
    lane = lax.broadcasted_iota(jnp.int32, (1, LANES), 1)
    f_base = hp * (HEADS_PER_BLOCK * F_COPIES)

    @pl.when(i == 0)
    def _():
        k2 = k_ref[...].astype(BF16)
        fk = fk_ref[...]
        for hh in range(HEADS_PER_BLOCK):
            kcat_ref[hh, :, 0:LANES] = k2
            kcat_ref[hh, :, LANES:2 * LANES] = _aug_k(fk, f_base + hh * F_COPIES, lane)
        vcat_ref[:, 0:LANES] = v_ref[...].astype(BF16)
        vcat_ref[:, LANES:2 * LANES] = jnp.ones((s_len, LANES), BF16)

    q2 = q_ref[...]
    fq = fq_ref[...]
    n_kv = ((i + 1) * tq + tk - 1) // tk
    dmat = lax.broadcasted_iota(jnp.int32, (tq, tk), 1) - lax.broadcasted_iota(jnp.int32, (tq, tk), 0)
    outs = []
    for hh in range(HEADS_PER_BLOCK):
        in_head = (lane >= hh * HEAD_DIM) & (lane < (hh + 1) * HEAD_DIM)
        qh = jnp.where(in_head, q2, jnp.zeros_like(q2))
        qcat = jnp.concatenate([qh, _aug_q(fq, f_base + hh * F_COPIES, lane)], axis=1)

        def body(j, carry, hh=hh, qcat=qcat):
            m, l, acc = carry
            off = pl.multiple_of(j * tk, tk)
            s = _dot_nt(qcat, kcat_ref[hh, pl.ds(off, tk), :])
            s = jnp.where(dmat <= i * tq - j * tk, s, NEG_INF)
            m_new = jnp.maximum(m, jnp.max(s, axis=1, keepdims=True))
            alpha = jnp.exp(m - m_new)
            p = jnp.exp(s - m_new).astype(BF16)
            pv = _dot(p, vcat_ref[pl.ds(off, tk), :])
            return m_new, alpha * l + pv[:, LANES:], alpha * acc + pv[:, :LANES]

        init = (jnp.full((tq, 1), NEG_INF, F32), jnp.zeros((tq, LANES), F32), jnp.zeros((tq, LANES), F32))
        _, l, acc = lax.fori_loop(0, n_kv, body, init)
        outs.append(acc / l)
    o_ref[...] = jnp.where(lane < HEAD_DIM, outs[0], outs[1]).astype(o_ref.dtype)


def _attn(q, k, v, f, *, batch, tq, tk):
    n, d = q.shape
    s = n // batch
    nq = s // tq
    n_hp = d // LANES
    q_spec = pl.BlockSpec((tq, LANES), lambda b, hp, i: (b * nq + i, hp))
    kv_spec = pl.BlockSpec((s, LANES), lambda b, hp, i: (b, hp))
    return pl.pallas_call(
        functools.partial(_attn_kernel, tq=tq, tk=tk),
        grid=(batch, n_hp, nq),
        in_specs=[q_spec, kv_spec, kv_spec,
                  pl.BlockSpec((tq, LANES), lambda b, hp, i: (b * nq + i, 0)),
                  pl.BlockSpec((s, LANES), lambda b, hp, i: (b, 0))],
        out_specs=q_spec,
        out_shape=jax.ShapeDtypeStruct((n, d), BF16),
        scratch_shapes=[pltpu.VMEM((HEADS_PER_BLOCK, s, 2 * LANES), BF16),
                        pltpu.VMEM((s, 2 * LANES), BF16)],
        compiler_params=pltpu.CompilerParams(
            dimension_semantics=("parallel", "parallel", "arbitrary"), vmem_limit_bytes=VMEM_LIMIT),
        name="attn",
    )(q, k, v, f, f)


def _decode_kernel(pt_ref, q_ref, kn_ref, vn_ref, lfn_ref, *rest, n_p):
    k_refs = rest[0:n_p]
    v_refs = rest[n_p:2 * n_p]
    lf_refs = rest[2 * n_p:3 * n_p]
    o_ref = rest[3 * n_p]
    qrow_ref, m_ref, l_ref, acc_ref, fqcol_ref, carry_ref = rest[3 * n_p + 1:]
    del pt_ref
    i = pl.program_id(1)
    t_new, d = q_ref.shape
    rows = N_HEADS * t_new
    n_grp = d // (2 * LANES)
    grp_rows = rows // n_grp
    lane = lax.broadcasted_iota(jnp.int32, (1, LANES), 1)
    row = lax.broadcasted_iota(jnp.int32, (rows, 1), 0)

    @pl.when(i == 0)
    def _():
        qt = jnp.concatenate([q_ref[...]] * N_HEADS, axis=0)
        lane_head = lax.broadcasted_iota(jnp.int32, (1, d), 1) // HEAD_DIM
        qrow = jnp.where(lane_head == row // t_new, qt, 0.0).astype(BF16)
        qrow_ref[...] = qrow
        fq = lfn_ref[...]
        sub = lax.broadcasted_iota(jnp.int32, (t_new, 1), 0)
        sh = 1
        while sh < t_new:
            fq = fq + jnp.where(sub >= sh, pltpu.roll(fq, sh, axis=0), 0.0)
            sh *= 2
        gt = jnp.concatenate([fq, jnp.zeros((LANES - t_new, LANES), F32)], axis=0).T
        t_of_row = row % t_new
        fqcol = jnp.sum(jnp.where(lane == t_of_row, gt, 0.0), axis=1, keepdims=True)
        fqcol_ref[...] = fqcol
        pad = jnp.zeros((LANES - t_new, d), F32)
        kn = jnp.concatenate([kn_ref[...], pad], axis=0).astype(BF16)
        vn = jnp.concatenate([vn_ref[...], pad], axis=0).astype(BF16)
        s = _dot_nt(qrow, kn) + fqcol - gt
        s = jnp.where(lane <= t_of_row, s, NEG_INF)
        m = jnp.max(s, axis=1, keepdims=True)
        p = jnp.exp(s - m)
        m_ref[...] = m
        l_ref[...] = jnp.sum(p, axis=1, keepdims=True)
        pb = p.astype(BF16)
        for g in range(n_grp):
            r0 = g * grp_rows
            acc_ref[r0:r0 + grp_rows, :] = _dot(pb[r0:r0 + grp_rows, :], vn[:, g * 2 * LANES:(g + 1) * 2 * LANES])
        carry_ref[...] = jnp.zeros(carry_ref.shape, F32)

    carry = carry_ref[...]
    biases = []
    for p_i in range(n_p):
        x = lf_refs[p_i][0]
        incl = x
        sh = 1
        while sh < PAGE_SIZE:
            incl = incl + jnp.where(lane < PAGE_SIZE - sh, pltpu.roll(incl, PAGE_SIZE - sh, axis=1), 0.0)
            sh *= 2
        r = incl - x + carry
        carry = carry + incl[:, 0:1]
        biases.append(jnp.broadcast_to(r[:, None, :], (N_HEADS, t_new, PAGE_SIZE)).reshape(rows, PAGE_SIZE))
    carry_ref[...] = carry

    kcat = jnp.concatenate([k_refs[p_i][0].astype(BF16) for p_i in range(n_p)], axis=0)
    s = _dot_nt(qrow_ref[...], kcat) + jnp.concatenate(biases, axis=1) + fqcol_ref[...]
    m_prev = m_ref[...]
    m_new = jnp.maximum(m_prev, jnp.max(s, axis=1, keepdims=True))
    alpha = jnp.exp(m_prev - m_new)
    p = jnp.exp(s - m_new)
    m_ref[...] = m_new
    l_ref[...] = alpha * l_ref[...] + jnp.sum(p, axis=1, keepdims=True)
    pb = p.astype(BF16)
    vcat = jnp.concatenate([v_refs[p_i][0].astype(BF16) for p_i in range(n_p)], axis=0)
    for g in range(n_grp):
        r0 = g * grp_rows
        acc_ref[r0:r0 + grp_rows, :] = (alpha[r0:r0 + grp_rows] * acc_ref[r0:r0 + grp_rows, :]
                                        + _dot(pb[r0:r0 + grp_rows, :], vcat[:, g * 2 * LANES:(g + 1) * 2 * LANES]))

    @pl.when(i == pl.num_programs(1) - 1)
    def _():
        o = acc_ref[...] / l_ref[...]
        lane2 = lax.broadcasted_iota(jnp.int32, (1, 2 * LANES), 1)
        heads_per_grp = grp_rows // t_new
        for g in range(n_grp):
            out = o[g * grp_rows:g * grp_rows + t_new, :]
            for hh in range(1, heads_per_grp):
                r0 = g * grp_rows + hh * t_new
                out = jnp.where(lane2 >= hh * HEAD_DIM, o[r0:r0 + t_new, :], out)
            o_ref[:, g * 2 * LANES:(g + 1) * 2 * LANES] = out


def _decode(page_table, q, kn, vn, lfn, cache_k, cache_v, cache_lft, *, n_p):
    db, n_pages = page_table.shape
    n, d = q.shape
    t_new = n // db
    rows = N_HEADS * t_new
    pt_flat = page_table.reshape(-1)

    def page_map(p_i):
        def index_map(b, i, pt):
            return (pt[b * n_pages + n_pages - 1 - (i * n_p + p_i)], 0, 0)
        return index_map

    tok = lambda w: pl.BlockSpec((t_new, w), lambda b, i, pt: (b, 0))
    k_specs = [pl.BlockSpec((1, PAGE_SIZE, d), page_map(p_i)) for p_i in range(n_p)]
    lf_specs = [pl.BlockSpec((1, N_HEADS, PAGE_SIZE), page_map(p_i)) for p_i in range(n_p)]
    grid_spec = pltpu.PrefetchScalarGridSpec(
        num_scalar_prefetch=1,
        grid=(db, n_pages // n_p),
        in_specs=[tok(d), tok(d), tok(d), tok(LANES)] + k_specs + k_specs + lf_specs,
        out_specs=tok(d),
        scratch_shapes=[pltpu.VMEM((rows, d), BF16), pltpu.VMEM((rows, 1), F32), pltpu.VMEM((rows, 1), F32),
                        pltpu.VMEM((rows, 2 * LANES), F32), pltpu.VMEM((rows, 1), F32),
                        pltpu.VMEM((N_HEADS, 1), F32)],
    )
    return pl.pallas_call(
        functools.partial(_decode_kernel, n_p=n_p),
        grid_spec=grid_spec,
        out_shape=jax.ShapeDtypeStruct((n, d), F32),
        compiler_params=pltpu.CompilerParams(dimension_semantics=("parallel", "arbitrary"),
                                             vmem_limit_bytes=VMEM_LIMIT),
        name="decode",
    )(pt_flat, q, kn, vn, lfn, *([cache_k] * n_p), *([cache_v] * n_p), *([cache_lft] * n_p))


def _window_mean_minus_cur(load, cnt_of, out_dtype):
    outs = []
    for g, w in enumerate(POOL_WINDOWS):
        lo = g * POOL_GROUP_IN
        cur = load(0, lo)
        win = cur
        for j in range(1, w):
            win = win + load(j, lo)
        outs.append((win / cnt_of(w) - cur).astype(out_dtype))
    return outs


def _pool_prompt_kernel(u_ref, o_ref, buf_ref, *, chunk):
    s = u_ref.shape[1]
    halo = buf_ref.shape[0] - s
    buf_ref[0:halo, :] = jnp.zeros((halo, buf_ref.shape[1]), F32)
    buf_ref[halo:halo + s, :] = u_ref[0]
    for c in range(s // chunk):
        r0 = c * chunk
        pos = (lax.broadcasted_iota(jnp.int32, (chunk, 1), 0) + r0).astype(F32)
        load = lambda j, lo, r0=r0: buf_ref[halo + r0 - j:halo + r0 - j + chunk, lo:lo + POOL_GROUP_IN]
        cnt_of = lambda w, pos=pos: jnp.minimum(pos + 1.0, float(w))
        for g, slab in enumerate(_window_mean_minus_cur(load, cnt_of, o_ref.dtype)):
            o_ref[0, r0:r0 + chunk, g * POOL_GROUP_IN:(g + 1) * POOL_GROUP_IN] = slab


def _pool_prompt(u, *, chunk):
    b, s, dp = u.shape
    spec = pl.BlockSpec((1, s, dp), lambda i: (i, 0, 0))
    return pl.pallas_call(
        functools.partial(_pool_prompt_kernel, chunk=chunk),
        grid=(b,),
        in_specs=[spec],
        out_specs=spec,
        out_shape=jax.ShapeDtypeStruct(u.shape, BF16),
        scratch_shapes=[pltpu.VMEM((s + 2 * SUBLANES, dp), F32)],
        compiler_params=pltpu.CompilerParams(dimension_semantics=("parallel",), vmem_limit_bytes=VMEM_LIMIT),
        name="pool_prompt",
    )(u)


def _pool_sample_kernel(u_ref, o_ref, *, past_len):
    db, ext, _ = u_ref.shape
    t_new = o_ref.shape[0] // db
    first = ext - t_new
    pos = (lax.broadcasted_iota(jnp.int32, (1, t_new, 1), 1) + past_len).astype(F32)
    load = lambda j, lo: u_ref[:, first - j:first - j + t_new, lo:lo + POOL_GROUP_IN]
    cnt_of = lambda w: jnp.minimum(pos + 1.0, float(w))
    for g, slab in enumerate(_window_mean_minus_cur(load, cnt_of, F32)):
        o_ref[:, g * POOL_GROUP_IN:(g + 1) * POOL_GROUP_IN] = slab.reshape(db * t_new, POOL_GROUP_IN).astype(o_ref.dtype)


def _pool_sample(u_ext, *, t_new, past_len):
    db, ext, dp = u_ext.shape
    return pl.pallas_call(
        functools.partial(_pool_sample_kernel, past_len=past_len),
        grid=(1,),
        in_specs=[_const_spec(u_ext.shape)],
        out_specs=_const_spec((db * t_new, dp)),
        out_shape=jax.ShapeDtypeStruct((db * t_new, dp), BF16),
        compiler_params=pltpu.CompilerParams(vmem_limit_bytes=VMEM_LIMIT),
        name="pool_sample",
    )(u_ext)


def _post_kernel(a_ref, p_ref, sga_ref, sgb_ref, x_ref, wpool_ref, ps_ref, wout_ref, gpost_ref,
                 gmlp_ref, wup_ref, wdown_ref, gmlp2_ref, y_ref):
    pooled = p_ref[...]
    parts = [_dot(pooled[:, g * POOL_GROUP_IN:(g + 1) * POOL_GROUP_IN], wpool_ref[g])
             for g in range(len(POOL_WINDOWS))]
    py = jnp.concatenate(parts, axis=1) * ps_ref[...]
    z = sga_ref[...].astype(F32) * a_ref[...].astype(F32) + sgb_ref[...].astype(F32) * py
    x1 = x_ref[...] + _rms(_dot(z.astype(BF16), wout_ref[...]), gpost_ref[...])
    h2 = _rms(x1, gmlp_ref[...]).astype(BF16)
    r = jnp.maximum(_dot(h2, wup_ref[...]), 0.0)
    dn = _dot((r * r).astype(BF16), wdown_ref[...])
    y_ref[...] = x1 + _rms(dn, gmlp2_ref[...])


def _post(a, pooled, sga, sgb, x, wpool, ps, wout, gpost, gmlp, wup, wdown, gmlp2, *, tm):
    n, d = x.shape
    row = lambda w: pl.BlockSpec((tm, w), lambda i: (i, 0))
    consts = (wpool, ps, wout, gpost, gmlp, wup, wdown, gmlp2)
    return pl.pallas_call(
        _post_kernel,
        grid=(n // tm,),
        in_specs=[row(d), row(pooled.shape[1]), row(d), row(d), row(d)] + [_const_spec(c.shape) for c in consts],
        out_specs=row(d),
        out_shape=jax.ShapeDtypeStruct((n, d), F32),
        compiler_params=pltpu.CompilerParams(dimension_semantics=("parallel",), vmem_limit_bytes=VMEM_LIMIT),
        name="post",
    )(a, pooled, sga, sgb, x, *consts)


def _pick(n, pref):
    t = min(n, pref)
    assert n % t == 0, (n, t)
    return t


def kernel(x_prompt, x_sample, cache_k, cache_v, cache_logf, state_pool, page_table, norm_attn_pre, norm_attn_post,
           w_in, b_forget, w_pool, pool_scale, w_out, norm_mlp_pre, norm_mlp_post, w_up, w_down):
    b, s, d = x_prompt.shape
    db, t_new, _ = x_sample.shape
    depth = w_in.shape[0]
    n_pages = page_table.shape[1]
    d_pool = state_pool.shape[-1]
    d_attn = N_HEADS * HEAD_DIM
    assert d == d_attn and d_pool == len(POOL_WINDOWS) * POOL_GROUP_IN and cache_k.shape[2] == PAGE_SIZE
    assert s % 256 == 0 and t_new == SUBLANES and N_HEADS * t_new == LANES
    n_p = 4 if n_pages % 4 == 0 else 1
    c_f = 3 * d_attn
    c_u = c_f + N_HEADS
    c_g = c_u + d_pool

    xp = x_prompt.reshape(b * s, d)
    xs = x_sample.reshape(db * t_new, d)
    row = lambda v: v.reshape(1, -1)
    outs = [[] for _ in range(8)]
    for dd in range(depth):
        w = w_in[dd]
        wqkv = w[:, :c_f].astype(BF16)
        wf = jnp.repeat(w[:, c_f:c_u], F_COPIES, axis=1).astype(BF16)
        bf = row(jnp.repeat(b_forget[dd], F_COPIES))
        wu = w[:, c_u:c_g].astype(BF16)
        wg = w[:, c_g:].astype(BF16)
        post_w = (w_pool[dd].astype(BF16), row(pool_scale[dd]), w_out[dd].astype(BF16), row(norm_attn_post[dd]),
                  row(norm_mlp_pre[dd]), w_up[dd].astype(BF16), w_down[dd].astype(BF16), row(norm_mlp_post[dd]))
        g_pre = row(norm_attn_pre[dd])

        q, k, v, lf, u, sga, sgb = _proj(xp, g_pre, wqkv, wf, bf, wu, wg, tm=_pick(b * s, 256))
        f = _cumsum(lf.reshape(b, s, LANES)).reshape(b * s, LANES)
        a = _attn(q, k, v, f, batch=b, tq=256, tk=256)
        pooled = _pool_prompt(u.reshape(b, s, d_pool), chunk=256).reshape(b * s, d_pool)
        xp = _post(a, pooled, sga, sgb, xp, *post_w, tm=_pick(b * s, 256))
        outs[0].append(k.reshape(b, s, N_HEADS, HEAD_DIM))
        outs[1].append(v.reshape(b, s, N_HEADS, HEAD_DIM))
        outs[2].append(lf[:, ::F_COPIES].reshape(b, s, N_HEADS))
        outs[3].append(u.reshape(b, s, d_pool)[:, s - POOL_STATE:])

        q, k, v, lf, u, sga, sgb = _proj(xs, g_pre, wqkv, wf, bf, wu, wg, tm=_pick(db * t_new, 256))
        ck = cache_k[dd].reshape(-1, PAGE_SIZE, d_attn)
        cv = cache_v[dd].reshape(-1, PAGE_SIZE, d_attn)
        clt = jnp.swapaxes(cache_logf[dd], 1, 2)
        a = _decode(page_table, q.astype(F32), k, v, lf, ck, cv, clt, n_p=n_p)
        u_ext = jnp.concatenate([jnp.zeros((db, 1, d_pool), F32), state_pool[dd], u.reshape(db, t_new, d_pool)], axis=1)
        pooled = _pool_sample(u_ext, t_new=t_new, past_len=n_pages * PAGE_SIZE)
        xs = _post(a, pooled, sga, sgb, xs, *post_w, tm=_pick(db * t_new, 256))
        outs[4].append(k.reshape(db, t_new, N_HEADS, HEAD_DIM))
        outs[5].append(v.reshape(db, t_new, N_HEADS, HEAD_DIM))
        outs[6].append(lf[:, ::F_COPIES].reshape(db, t_new, N_HEADS))
        outs[7].append(u_ext[:, -POOL_STATE:])

    stacked = [jnp.stack(o) for o in outs]
    return (xp.reshape(b, s, d), xs.reshape(db, t_new, d), *stacked)
```

```python
import functools
import math

import jax
import jax.numpy as jnp
from jax import lax
from jax.experimental import pallas as pl
from jax.experimental.pallas import tpu as pltpu

N_HEADS = 16
HEAD_DIM = 64
POOL_WINDOWS = (2, 4, 8, 16)
POOL_STATE = max(POOL_WINDOWS) - 1
POOL_GROUP_IN = 128
POOL_GROUP_OUT = 256
RMS_EPS = 1e-6
PAGE_SIZE = 128
SCALE = 1.0 / math.sqrt(HEAD_DIM)
LOG2E = math.log2(math.e)

LANES = 128
SUBLANES = 8
BF16_ROWS = 16
HEADS_PER_BLOCK = LANES // HEAD_DIM
F_COPIES = LANES // N_HEADS
VMEM_LIMIT = 56 * 1024 * 1024

F32 = jnp.float32
BF16 = jnp.bfloat16
NEG_INF = float("-inf")

_dot = functools.partial(jnp.dot, preferred_element_type=F32)


def _dot_nt(a, b):
    return lax.dot_general(a, b, (((1,), (1,)), ((), ())), preferred_element_type=F32)


def _rms(x, g):
    return x * lax.rsqrt(jnp.mean(x * x, axis=-1, keepdims=True) + RMS_EPS) * g


def _sigmoid(x):
    return 1.0 / (1.0 + jnp.exp(-x))


def _log_sigmoid(x):
    return jnp.minimum(x, 0.0) - jnp.log1p(jnp.exp(-jnp.abs(x)))


def _split3(f):
    hi = f.astype(BF16).astype(F32)
    r = f - hi
    mid = r.astype(BF16).astype(F32)
    lo = (r - mid).astype(BF16).astype(F32)
    return hi, mid, lo


def _const_spec(shape):
    return pl.BlockSpec(shape, lambda *_: (0,) * len(shape))


def _proj_kernel(x_ref, g_ref, wq_ref, wk_ref, wv_ref, wf_ref, bf_ref, wu_ref, wg_ref,
                 q_ref, k_ref, v_ref, lf_ref, u_ref, sga_ref, sgb_ref, *, kv_transposed, q_scale):
    d = x_ref.shape[1]
    h = _rms(x_ref[...], g_ref[...]).astype(BF16)
    q_ref[...] = (_dot(h, wq_ref[...]) * q_scale).astype(BF16)
    if kv_transposed:
        k_ref[0] = _dot_nt(wk_ref[...], h)
        v_ref[0] = _dot_nt(wv_ref[...], h)
    else:
        k_ref[...] = _dot(h, wk_ref[...])
        v_ref[...] = _dot(h, wv_ref[...])
    lf_ref[...] = _log_sigmoid(_dot(h, wf_ref[...]) + bf_ref[...])
    u_ref[...] = _dot(h, wu_ref[...])
    sga_ref[...] = _sigmoid(_dot(h, wg_ref[:, 0:d])).astype(BF16)
    sgb_ref[...] = _sigmoid(_dot(h, wg_ref[:, d:2 * d])).astype(BF16)


def _proj(x, g, wq, wk, wv, wf, bf, wu, wg, *, tm, q_scale, seq_len=None):
    n, d = x.shape
    d_pool = wu.shape[1]
    row = lambda w: pl.BlockSpec((tm, w), lambda i: (i, 0))
    if seq_len is None:
        kv_spec = row(d)
        kv_shape = jax.ShapeDtypeStruct((n, d), F32)
    else:
        per_seq = seq_len // tm
        kv_spec = pl.BlockSpec((1, d, tm), lambda i: (i // per_seq, 0, i % per_seq))
        kv_shape = jax.ShapeDtypeStruct((n // seq_len, d, seq_len), F32)
    consts = (g, wq, wk, wv, wf, bf, wu, wg)
    return pl.pallas_call(
        functools.partial(_proj_kernel, kv_transposed=seq_len is not None, q_scale=q_scale),
        grid=(n // tm,),
        in_specs=[row(d)] + [_const_spec(c.shape) for c in consts],
        out_specs=[row(d), kv_spec, kv_spec, row(LANES), row(d_pool), row(d), row(d)],
        out_shape=[jax.ShapeDtypeStruct((n, d), BF16), kv_shape, kv_shape,
                   jax.ShapeDtypeStruct((n, LANES), F32), jax.ShapeDtypeStruct((n, d_pool), F32),
                   jax.ShapeDtypeStruct((n, d), BF16), jax.ShapeDtypeStruct((n, d), BF16)],
        compiler_params=pltpu.CompilerParams(dimension_semantics=("parallel",),
                                             vmem_limit_bytes=VMEM_LIMIT),
        name="proj",
    )(x, *consts)


def _cumsum_kernel(lf_ref, f_ref, ft_ref, lft_ref, buf_ref, t_ref):
    s = lf_ref.shape[1]
    pad = buf_ref.shape[0] - s
    lf = lf_ref[0]
    buf_ref[0:pad, :] = jnp.zeros((pad, LANES), F32)
    buf_ref[pad:pad + s, :] = lf
    d = 1
    while d < s:
        y = buf_ref[pad:pad + s, :] + buf_ref[pad - d:pad - d + s, :]
        buf_ref[pad:pad + s, :] = y
        d *= 2
    f = buf_ref[pad:pad + s, :]
    f_ref[0] = f
    for c in range(s // LANES):
        rows = slice(pad + c * LANES, pad + (c + 1) * LANES)
        t_ref[...] = buf_ref[rows, :].T
        ft_ref[0, :, c * LANES:(c + 1) * LANES] = t_ref[pl.ds(0, N_HEADS, stride=F_COPIES), :]
        t_ref[...] = lf_ref[0, c * LANES:(c + 1) * LANES, :].T
        lft_ref[0, :, c * LANES:(c + 1) * LANES] = t_ref[pl.ds(0, N_HEADS, stride=F_COPIES), :]


def _cumsum(lf):
    b, s, _ = lf.shape
    spec = pl.BlockSpec((1, s, LANES), lambda i: (i, 0, 0))
    spec_t = pl.BlockSpec((1, N_HEADS, s), lambda i: (i, 0, 0))
    shape_t = jax.ShapeDtypeStruct((b, N_HEADS, s), F32)
    return pl.pallas_call(
        _cumsum_kernel,
        grid=(b,),
        in_specs=[spec],
        out_specs=[spec, spec_t, spec_t],
        out_shape=[jax.ShapeDtypeStruct(lf.shape, F32), shape_t, shape_t],
        scratch_shapes=[pltpu.VMEM((s + s // 2, LANES), F32), pltpu.VMEM((LANES, LANES), F32)],
        compiler_params=pltpu.CompilerParams(dimension_semantics=("parallel",)),
        name="cumsum",
    )(lf)


def _aug_q(f, base, lane):
    hi, mid, lo = _split3(f)
    rel = lane - base
    one = jnp.where((rel >= 3) & (rel < 6), 1.0, 0.0)
    return jnp.where(rel == 0, hi, jnp.where(rel == 1, mid, jnp.where(rel == 2, lo, one))).astype(BF16)


def _aug_k_rows(f_row, first):
    hi, mid, lo = _split3(f_row)
    rel = lax.broadcasted_iota(jnp.int32, (BF16_ROWS, 1), 0) - first
    one = jnp.where((rel >= 0) & (rel < 3), 1.0, 0.0)
    return jnp.where(rel == 3, -hi, jnp.where(rel == 4, -mid, jnp.where(rel == 5, -lo, one))).astype(BF16)


def _attn_kernel(q_ref, k_ref, v_ref, fq_ref, fk_ref, o_ref, kcat_ref, vcat_ref, *, t):
    hp = pl.program_id(1)
    i = pl.program_id(2)
    n_t = kcat_ref.shape[1]
    lane = lax.broadcasted_iota(jnp.int32, (1, LANES), 1)
    f_base = hp * (HEADS_PER_BLOCK * F_COPIES)

    @pl.when(i == 0)
    def _():
        aug_first = pl.multiple_of(LANES + f_base, BF16_ROWS)
        dim_row = lax.broadcasted_iota(jnp.int32, (LANES, 1), 0)
        for j in range(n_t):
            k2 = k_ref[0, :, j * t:(j + 1) * t].astype(BF16)
            v2 = v_ref[0, :, j * t:(j + 1) * t]
            for hh in range(HEADS_PER_BLOCK):
                kcat_ref[hh, j, 0:LANES, :] = k2
                kcat_ref[hh, j, LANES:2 * LANES, :] = jnp.zeros((LANES, t), BF16)
                fk = fk_ref[0, pl.ds(hp * HEADS_PER_BLOCK + hh, 1), j * t:(j + 1) * t] * LOG2E
                kcat_ref[hh, j, pl.ds(aug_first, BF16_ROWS), :] = _aug_k_rows(fk, hh * F_COPIES)
                own = (dim_row >= hh * HEAD_DIM) & (dim_row < (hh + 1) * HEAD_DIM)
                vcat_ref[hh, j] = jnp.where(own, v2, 1.0).astype(BF16)

    q2 = q_ref[...]
    fq = fq_ref[...] * LOG2E
    qcats = []
    for hh in range(HEADS_PER_BLOCK):
        in_head = (lane >= hh * HEAD_DIM) & (lane < (hh + 1) * HEAD_DIM)
        qh = jnp.where(in_head, q2, jnp.zeros_like(q2))
        qcats.append(jnp.concatenate([qh, _aug_q(fq, f_base + hh * F_COPIES, lane)], axis=1))

    dmat = lax.broadcasted_iota(jnp.int32, (t, t), 1) - lax.broadcasted_iota(jnp.int32, (t, t), 0)

    def scores(j):
        jc = jnp.minimum(j, n_t - 1)
        return tuple(_dot(qcats[hh], kcat_ref[hh, jc]) for hh in range(HEADS_PER_BLOCK))

    def consume(j, s_both, state):
        jc = jnp.minimum(j, n_t - 1)
        keep = dmat <= (i - j) * t
        new = []
        for hh in range(HEADS_PER_BLOCK):
            m, acc = state[hh]
            s = jnp.where(keep, s_both[hh], NEG_INF)
            m_new = jnp.maximum(m, jnp.max(s, axis=1, keepdims=True))
            alpha = jnp.exp2(m - m_new)
            p = jnp.exp2(s - m_new).astype(BF16)
            new.append((m_new, alpha * acc + _dot_nt(p, vcat_ref[hh, jc])))
        return tuple(new)

    def pair(jj, carry):
        state, s0, s1 = carry
        n0 = scores(2 * jj + 2)
        n1 = scores(2 * jj + 3)
        state = consume(2 * jj, s0, state)
        state = consume(2 * jj + 1, s1, state)
        return state, n0, n1

    init = tuple((jnp.full((t, 1), NEG_INF, F32), jnp.zeros((t, LANES), F32)) for _ in range(HEADS_PER_BLOCK))
    state, _, _ = lax.fori_loop(0, (i + 2) // 2, pair, (init, scores(0), scores(1)))
    outs = [acc / pltpu.roll(acc, HEAD_DIM, axis=1) for _, acc in state]
    o_ref[...] = jnp.where(lane < HEAD_DIM, outs[0], outs[1]).astype(o_ref.dtype)


def _attn(q, kt, vt, f, ft, *, t):
    n, d = q.shape
    batch, _, s = kt.shape
    n_t = s // t
    n_hp = d // LANES
    q_spec = pl.BlockSpec((t, LANES), lambda b, hp, i: (b * n_t + i, hp))
    kv_spec = pl.BlockSpec((1, LANES, s), lambda b, hp, i: (b, hp, 0))
    return pl.pallas_call(
        functools.partial(_attn_kernel, t=t),
        grid=(batch, n_hp, n_t),
        in_specs=[q_spec, kv_spec, kv_spec,
                  pl.BlockSpec((t, LANES), lambda b, hp, i: (b * n_t + i, 0)),
                  pl.BlockSpec((1, N_HEADS, s), lambda b, hp, i: (b, 0, 0))],
        out_specs=q_spec,
        out_shape=jax.ShapeDtypeStruct((n, d), BF16),
        scratch_shapes=[pltpu.VMEM((HEADS_PER_BLOCK, n_t, 2 * LANES, t), BF16),
                        pltpu.VMEM((HEADS_PER_BLOCK, n_t, LANES, t), BF16)],
        compiler_params=pltpu.CompilerParams(
            dimension_semantics=("parallel", "parallel", "arbitrary"), vmem_limit_bytes=VMEM_LIMIT),
        name="attn",
    )(q, kt, vt, f, ft)


def _decode_kernel(pt_ref, q_ref, kn_ref, vn_ref, lfn_ref, *rest, n_p):
    k_refs = rest[0:n_p]
    v_refs = rest[n_p:2 * n_p]
    lf_refs = rest[2 * n_p:3 * n_p]
    o_ref = rest[3 * n_p]
    qrow_ref, m_ref, l_ref, acc_ref, fqcol_ref, carry_ref = rest[3 * n_p + 1:]
    del pt_ref
    i = pl.program_id(1)
    t_new, d = q_ref.shape
    rows = N_HEADS * t_new
    n_grp = d // (2 * LANES)
    grp_rows = rows // n_grp
    lane = lax.broadcasted_iota(jnp.int32, (1, LANES), 1)
    row = lax.broadcasted_iota(jnp.int32, (rows, 1), 0)

    @pl.when(i == 0)
    def _():
        qt = jnp.concatenate([q_ref[...]] * N_HEADS, axis=0)
        lane_head = lax.broadcasted_iota(jnp.int32, (1, d), 1) // HEAD_DIM
        qrow = jnp.where(lane_head == row // t_new, qt, 0.0).astype(BF16)
        qrow_ref[...] = qrow
        fq = lfn_ref[...]
        sub = lax.broadcasted_iota(jnp.int32, (t_new, 1), 0)
        sh = 1
        while sh < t_new:
            fq = fq + jnp.where(sub >= sh, pltpu.roll(fq, sh, axis=0), 0.0)
            sh *= 2
        gt = jnp.concatenate([fq, jnp.zeros((LANES - t_new, LANES), F32)], axis=0).T
        t_of_row = row % t_new
        fqcol = jnp.sum(jnp.where(lane == t_of_row, gt, 0.0), axis=1, keepdims=True)
        fqcol_ref[...] = fqcol
        pad = jnp.zeros((LANES - t_new, d), F32)
        kn = jnp.concatenate([kn_ref[...], pad], axis=0).astype(BF16)
        vn = jnp.concatenate([vn_ref[...], pad], axis=0).astype(BF16)
        s = _dot_nt(qrow, kn) + fqcol - gt
        s = jnp.where(lane <= t_of_row, s, NEG_INF)
        m = jnp.max(s, axis=1, keepdims=True)
        p = jnp.exp(s - m)
        m_ref[...] = m
        l_ref[...] = jnp.sum(p, axis=1, keepdims=True)
        pb = p.astype(BF16)
        for g in range(n_grp):
            r0 = g * grp_rows
            acc_ref[r0:r0 + grp_rows, :] = _dot(pb[r0:r0 + grp_rows, :], vn[:, g * 2 * LANES:(g + 1) * 2 * LANES])
        carry_ref[...] = jnp.zeros(carry_ref.shape, F32)

    carry = carry_ref[...]
    biases = []
    for p_i in range(n_p):
        x = lf_refs[p_i][0]
        incl = x
        sh = 1
        while sh < PAGE_SIZE:
            incl = incl + jnp.where(lane < PAGE_SIZE - sh, pltpu.roll(incl, PAGE_SIZE - sh, axis=1), 0.0)
            sh *= 2
        r = incl - x + carry
        carry = carry + incl[:, 0:1]
        biases.append(jnp.broadcast_to(r[:, None, :], (N_HEADS, t_new, PAGE_SIZE)).reshape(rows, PAGE_SIZE))
    carry_ref[...] = carry

    ktcat = jnp.concatenate([k_refs[p_i][0].astype(BF16) for p_i in range(n_p)], axis=1)
    s = _dot(qrow_ref[...], ktcat) + jnp.concatenate(biases, axis=1) + fqcol_ref[...]
    m_prev = m_ref[...]
    m_new = jnp.maximum(m_prev, jnp.max(s, axis=1, keepdims=True))
    alpha = jnp.exp(m_prev - m_new)
    p = jnp.exp(s - m_new)
    m_ref[...] = m_new
    l_ref[...] = alpha * l_ref[...] + jnp.sum(p, axis=1, keepdims=True)
    pb = p.astype(BF16)
    for g in range(n_grp):
        r0 = g * grp_rows
        c0 = g * 2 * LANES
        vt = jnp.concatenate([v_refs[p_i][0, c0:c0 + 2 * LANES, :].astype(BF16) for p_i in range(n_p)], axis=1)
        acc_ref[r0:r0 + grp_rows, :] = (alpha[r0:r0 + grp_rows] * acc_ref[r0:r0 + grp_rows, :]
                                        + _dot_nt(pb[r0:r0 + grp_rows, :], vt))

    @pl.when(i == pl.num_programs(1) - 1)
    def _():
        o = acc_ref[...] / l_ref[...]
        lane2 = lax.broadcasted_iota(jnp.int32, (1, 2 * LANES), 1)
        heads_per_grp = grp_rows // t_new
        for g in range(n_grp):
            out = o[g * grp_rows:g * grp_rows + t_new, :]
            for hh in range(1, heads_per_grp):
                r0 = g * grp_rows + hh * t_new
                out = jnp.where(lane2 >= hh * HEAD_DIM, o[r0:r0 + t_new, :], out)
            o_ref[:, g * 2 * LANES:(g + 1) * 2 * LANES] = out


def _decode(page_table, q, kn, vn, lfn, cache_kt, cache_vt, cache_lft, *, n_p):
    db, n_pages = page_table.shape
    n, d = q.shape
    t_new = n // db
    rows = N_HEADS * t_new
    pt_flat = page_table.reshape(-1)

    def page_map(p_i):
        def index_map(b, i, pt):
            return (pt[b * n_pages + n_pages - 1 - (i * n_p + p_i)], 0, 0)
        return index_map

    tok = lambda w: pl.BlockSpec((t_new, w), lambda b, i, pt: (b, 0))
    k_specs = [pl.BlockSpec((1, d, PAGE_SIZE), page_map(p_i)) for p_i in range(n_p)]
    lf_specs = [pl.BlockSpec((1, N_HEADS, PAGE_SIZE), page_map(p_i)) for p_i in range(n_p)]
    grid_spec = pltpu.PrefetchScalarGridSpec(
        num_scalar_prefetch=1,
        grid=(db, n_pages // n_p),
        in_specs=[tok(d), tok(d), tok(d), tok(LANES)] + k_specs + k_specs + lf_specs,
        out_specs=tok(d),
        scratch_shapes=[pltpu.VMEM((rows, d), BF16), pltpu.VMEM((rows, 1), F32), pltpu.VMEM((rows, 1), F32),
                        pltpu.VMEM((rows, 2 * LANES), F32), pltpu.VMEM((rows, 1), F32),
                        pltpu.VMEM((N_HEADS, 1), F32)],
    )
    return pl.pallas_call(
        functools.partial(_decode_kernel, n_p=n_p),
        grid_spec=grid_spec,
        out_shape=jax.ShapeDtypeStruct((n, d), F32),
        compiler_params=pltpu.CompilerParams(dimension_semantics=("parallel", "arbitrary"),
                                             vmem_limit_bytes=VMEM_LIMIT),
        name="decode",
    )(pt_flat, q, kn, vn, lfn, *([cache_kt] * n_p), *([cache_vt] * n_p), *([cache_lft] * n_p))


def _window_mean_minus_cur(load, cnt_of, out_dtype):
    outs = []
    for g, w in enumerate(POOL_WINDOWS):
        lo = g * POOL_GROUP_IN
        cur = load(0, lo)
        win = cur
        for j in range(1, w):
            win = win + load(j, lo)
        outs.append((win / cnt_of(w) - cur).astype(out_dtype))
    return outs


def _pool_prompt_kernel(u_ref, o_ref, buf_ref, *, chunk):
    s = u_ref.shape[1]
    halo = buf_ref.shape[0] - s
    buf_ref[0:halo, :] = jnp.zeros((halo, buf_ref.shape[1]), F32)
    buf_ref[halo:halo + s, :] = u_ref[0]
    for c in range(s // chunk):
        r0 = c * chunk
        pos = (lax.broadcasted_iota(jnp.int32, (chunk, 1), 0) + r0).astype(F32)
        load = lambda j, lo, r0=r0: buf_ref[halo + r0 - j:halo + r0 - j + chunk, lo:lo + POOL_GROUP_IN]
        cnt_of = lambda w, pos=pos: jnp.minimum(pos + 1.0, float(w))
        for g, slab in enumerate(_window_mean_minus_cur(load, cnt_of, o_ref.dtype)):
            o_ref[0, r0:r0 + chunk, g * POOL_GROUP_IN:(g + 1) * POOL_GROUP_IN] = slab


def _pool_prompt(u, *, chunk):
    b, s, dp = u.shape
    spec = pl.BlockSpec((1, s, dp), lambda i: (i, 0, 0))
    return pl.pallas_call(
        functools.partial(_pool_prompt_kernel, chunk=chunk),
        grid=(b,),
        in_specs=[spec],
        out_specs=spec,
        out_shape=jax.ShapeDtypeStruct(u.shape, BF16),
        scratch_shapes=[pltpu.VMEM((s + 2 * SUBLANES, dp), F32)],
        compiler_params=pltpu.CompilerParams(dimension_semantics=("parallel",), vmem_limit_bytes=VMEM_LIMIT),
        name="pool_prompt",
    )(u)


def _pool_sample_kernel(u_ref, o_ref, *, past_len):
    db, ext, _ = u_ref.shape
    t_new = o_ref.shape[0] // db
    first = ext - t_new
    pos = (lax.broadcasted_iota(jnp.int32, (1, t_new, 1), 1) + past_len).astype(F32)
    load = lambda j, lo: u_ref[:, first - j:first - j + t_new, lo:lo + POOL_GROUP_IN]
    cnt_of = lambda w: jnp.minimum(pos + 1.0, float(w))
    for g, slab in enumerate(_window_mean_minus_cur(load, cnt_of, F32)):
        o_ref[:, g * POOL_GROUP_IN:(g + 1) * POOL_GROUP_IN] = slab.reshape(db * t_new, POOL_GROUP_IN).astype(o_ref.dtype)


def _pool_sample(u_ext, *, t_new, past_len):
    db, ext, dp = u_ext.shape
    return pl.pallas_call(
        functools.partial(_pool_sample_kernel, past_len=past_len),
        grid=(1,),
        in_specs=[_const_spec(u_ext.shape)],
        out_specs=_const_spec((db * t_new, dp)),
        out_shape=jax.ShapeDtypeStruct((db * t_new, dp), BF16),
        compiler_params=pltpu.CompilerParams(vmem_limit_bytes=VMEM_LIMIT),
        name="pool_sample",
    )(u_ext)


def _post_kernel(a_ref, p_ref, sga_ref, sgb_ref, x_ref, wpool_ref, ps_ref, wout_ref, gpost_ref,
                 gmlp_ref, wup_ref, wdown_ref, gmlp2_ref, y_ref):
    pooled = p_ref[...]
    parts = [_dot(pooled[:, g * POOL_GROUP_IN:(g + 1) * POOL_GROUP_IN], wpool_ref[g])
             for g in range(len(POOL_WINDOWS))]
    py = jnp.concatenate(parts, axis=1) * ps_ref[...]
    z = sga_ref[...].astype(F32) * a_ref[...].astype(F32) + sgb_ref[...].astype(F32) * py
    x1 = x_ref[...] + _rms(_dot(z.astype(BF16), wout_ref[...]), gpost_ref[...])
    h2 = _rms(x1, gmlp_ref[...]).astype(BF16)
    r = jnp.maximum(_dot(h2, wup_ref[...]), 0.0)
    dn = _dot((r * r).astype(BF16), wdown_ref[...])
    y_ref[...] = x1 + _rms(dn, gmlp2_ref[...])


def _post(a, pooled, sga, sgb, x, wpool, ps, wout, gpost, gmlp, wup, wdown, gmlp2, *, tm):
    n, d = x.shape
    row = lambda w: pl.BlockSpec((tm, w), lambda i: (i, 0))
    consts = (wpool, ps, wout, gpost, gmlp, wup, wdown, gmlp2)
    return pl.pallas_call(
        _post_kernel,
        grid=(n // tm,),
        in_specs=[row(d), row(pooled.shape[1]), row(d), row(d), row(d)] + [_const_spec(c.shape) for c in consts],
        out_specs=row(d),
        out_shape=jax.ShapeDtypeStruct((n, d), F32),
        compiler_params=pltpu.CompilerParams(dimension_semantics=("parallel",), vmem_limit_bytes=VMEM_LIMIT),
        name="post",
    )(a, pooled, sga, sgb, x, *consts)


def _pick(n, pref):
    t = min(n, pref)
    assert n % t == 0, (n, t)
    return t


def _head_major(x):
    b, hd, s = x.shape
    if hd == N_HEADS:
        return jnp.transpose(x, (0, 2, 1))
    return jnp.transpose(x.reshape(b, N_HEADS, HEAD_DIM, s), (0, 3, 1, 2))


def kernel(x_prompt, x_sample, cache_k, cache_v, cache_logf, state_pool, page_table, norm_attn_pre, norm_attn_post,
           w_in, b_forget, w_pool, pool_scale, w_out, norm_mlp_pre, norm_mlp_post, w_up, w_down):
    b, s, d = x_prompt.shape
    db, t_new, _ = x_sample.shape
    depth = w_in.shape[0]
    n_pages = page_table.shape[1]
    n_phys = cache_k.shape[1]
    d_pool = state_pool.shape[-1]
    d_attn = N_HEADS * HEAD_DIM
    assert d == d_attn and d_pool == len(POOL_WINDOWS) * POOL_GROUP_IN and cache_k.shape[2] == PAGE_SIZE
    assert s % 256 == 0 and t_new == SUBLANES and N_HEADS * t_new == LANES
    n_p = max(p for p in (8, 4, 2, 1) if n_pages % p == 0)
    c_f = 3 * d_attn
    c_u = c_f + N_HEADS
    c_g = c_u + d_pool

    xp = x_prompt.reshape(b * s, d)
    xs = x_sample.reshape(db * t_new, d)
    row = lambda v: v.reshape(1, -1)
    outs = [[] for _ in range(8)]
    for dd in range(depth):
        w = w_in[dd]
        wq = w[:, :d_attn].astype(BF16)
        wk = w[:, d_attn:2 * d_attn].astype(BF16)
        wv = w[:, 2 * d_attn:c_f].astype(BF16)
        wf = jnp.repeat(w[:, c_f:c_u], F_COPIES, axis=1).astype(BF16)
        bf = row(jnp.repeat(b_forget[dd], F_COPIES))
        wu = w[:, c_u:c_g].astype(BF16)
        wg = w[:, c_g:].astype(BF16)
        post_w = (w_pool[dd].astype(BF16), row(pool_scale[dd]), w_out[dd].astype(BF16), row(norm_attn_post[dd]),
                  row(norm_mlp_pre[dd]), w_up[dd].astype(BF16), w_down[dd].astype(BF16), row(norm_mlp_post[dd]))
        g_pre = row(norm_attn_pre[dd])

        q, kt, vt, lf, u, sga, sgb = _proj(xp, g_pre, wq, wk.T, wv.T, wf, bf, wu, wg, tm=_pick(s, 256),
                                           q_scale=SCALE * LOG2E, seq_len=s)
        f, ft, lft = _cumsum(lf.reshape(b, s, LANES))
        a = _attn(q, kt, vt, f.reshape(b * s, LANES), ft, t=256)
        pooled = _pool_prompt(u.reshape(b, s, d_pool), chunk=256).reshape(b * s, d_pool)
        xp = _post(a, pooled, sga, sgb, xp, *post_w, tm=_pick(b * s, 256))
        outs[0].append(_head_major(kt))
        outs[1].append(_head_major(vt))
        outs[2].append(_head_major(lft))
        outs[3].append(u.reshape(b, s, d_pool)[:, s - POOL_STATE:])

        q, k, v, lf, u, sga, sgb = _proj(xs, g_pre, wq, wk, wv, wf, bf, wu, wg, tm=_pick(db * t_new, 256),
                                         q_scale=SCALE)
        ckt = jnp.transpose(cache_k[dd], (0, 2, 3, 1)).reshape(n_phys, d_attn, PAGE_SIZE)
        cvt = jnp.transpose(cache_v[dd], (0, 2, 3, 1)).reshape(n_phys, d_attn, PAGE_SIZE)
        clt = jnp.swapaxes(cache_logf[dd], 1, 2)
        a = _decode(page_table, q.astype(F32), k, v, lf, ckt, cvt, clt, n_p=n_p)
        u_ext = jnp.concatenate([jnp.zeros((db, 1, d_pool), F32), state_pool[dd], u.reshape(db, t_new, d_pool)], axis=1)
        pooled = _pool_sample(u_ext, t_new=t_new, past_len=n_pages * PAGE_SIZE)
        xs = _post(a, pooled, sga, sgb, xs, *post_w, tm=_pick(db * t_new, 256))
        outs[4].append(k.reshape(db, t_new, N_HEADS, HEAD_DIM))
        outs[5].append(v.reshape(db, t_new, N_HEADS, HEAD_DIM))
        outs[6].append(lf[:, ::F_COPIES].reshape(db, t_new, N_HEADS))
        outs[7].append(u_ext[:, -POOL_STATE:])

    stacked = [jnp.stack(o) for o in outs]
    return (xp.reshape(b, s, d), xs.reshape(db, t_new, d), *stacked)
```

```python
import functools
import math

import jax
import jax.numpy as jnp
from jax import lax
from jax.experimental import pallas as pl
from jax.experimental.pallas import tpu as pltpu

N_HEADS = 16
HEAD_DIM = 64
POOL_WINDOWS = (2, 4, 8, 16)
POOL_STATE = max(POOL_WINDOWS) - 1
POOL_GROUP_IN = 128
POOL_GROUP_OUT = 256
RMS_EPS = 1e-6
PAGE_SIZE = 128
SCALE = 1.0 / math.sqrt(HEAD_DIM)
LOG2E = math.log2(math.e)

LANES = 128
SUBLANES = 8
BF16_ROWS = 16
HEADS_PER_BLOCK = LANES // HEAD_DIM
F_COPIES = LANES // N_HEADS
VMEM_LIMIT = 56 * 1024 * 1024

F32 = jnp.float32
BF16 = jnp.bfloat16
NEG_INF = float("-inf")

_dot = functools.partial(jnp.dot, preferred_element_type=F32)


def _dot_nt(a, b):
    return lax.dot_general(a, b, (((1,), (1,)), ((), ())), preferred_element_type=F32)


def _rms(x, g):
    return x * lax.rsqrt(jnp.mean(x * x, axis=-1, keepdims=True) + RMS_EPS) * g


def _sigmoid(x):
    return 1.0 / (1.0 + jnp.exp(-x))


def _log_sigmoid(x):
    return jnp.minimum(x, 0.0) - jnp.log1p(jnp.exp(-jnp.abs(x)))


def _split3(f):
    hi = f.astype(BF16).astype(F32)
    r = f - hi
    mid = r.astype(BF16).astype(F32)
    lo = (r - mid).astype(BF16).astype(F32)
    return hi, mid, lo


def _const_spec(shape):
    return pl.BlockSpec(shape, lambda *_: (0,) * len(shape))


def _proj_kernel(x_ref, g_ref, wq_ref, wk_ref, wv_ref, wf_ref, bf_ref, wu_ref, wg_ref,
                 q_ref, k_ref, v_ref, lf_ref, u_ref, sga_ref, sgb_ref, *, kv_transposed, q_scale):
    d = x_ref.shape[1]
    h = _rms(x_ref[...], g_ref[...]).astype(BF16)
    q_ref[...] = (_dot(h, wq_ref[...]) * q_scale).astype(BF16)
    if kv_transposed:
        k_ref[0] = _dot_nt(wk_ref[...], h)
        v_ref[0] = _dot_nt(wv_ref[...], h)
    else:
        k_ref[...] = _dot(h, wk_ref[...])
        v_ref[...] = _dot(h, wv_ref[...])
    lf_ref[...] = _log_sigmoid(_dot(h, wf_ref[...]) + bf_ref[...])
    u_ref[...] = _dot(h, wu_ref[...])
    sga_ref[...] = _sigmoid(_dot(h, wg_ref[:, 0:d])).astype(BF16)
    sgb_ref[...] = _sigmoid(_dot(h, wg_ref[:, d:2 * d])).astype(BF16)


def _proj(x, g, wq, wk, wv, wf, bf, wu, wg, *, tm, q_scale, seq_len=None):
    n, d = x.shape
    d_pool = wu.shape[1]
    row = lambda w: pl.BlockSpec((tm, w), lambda i: (i, 0))
    if seq_len is None:
        kv_spec = row(d)
        kv_shape = jax.ShapeDtypeStruct((n, d), F32)
    else:
        per_seq = seq_len // tm
        kv_spec = pl.BlockSpec((1, d, tm), lambda i: (i // per_seq, 0, i % per_seq))
        kv_shape = jax.ShapeDtypeStruct((n // seq_len, d, seq_len), F32)
    consts = (g, wq, wk, wv, wf, bf, wu, wg)
    return pl.pallas_call(
        functools.partial(_proj_kernel, kv_transposed=seq_len is not None, q_scale=q_scale),
        grid=(n // tm,),
        in_specs=[row(d)] + [_const_spec(c.shape) for c in consts],
        out_specs=[row(d), kv_spec, kv_spec, row(LANES), row(d_pool), row(d), row(d)],
        out_shape=[jax.ShapeDtypeStruct((n, d), BF16), kv_shape, kv_shape,
                   jax.ShapeDtypeStruct((n, LANES), F32), jax.ShapeDtypeStruct((n, d_pool), F32),
                   jax.ShapeDtypeStruct((n, d), BF16), jax.ShapeDtypeStruct((n, d), BF16)],
        compiler_params=pltpu.CompilerParams(dimension_semantics=("parallel",),
                                             vmem_limit_bytes=VMEM_LIMIT),
        name="proj",
    )(x, *consts)


def _cumsum_kernel(lf_ref, f_ref, ft_ref, lft_ref, buf_ref, t_ref):
    s = lf_ref.shape[1]
    pad = buf_ref.shape[0] - s
    lf = lf_ref[0]
    buf_ref[0:pad, :] = jnp.zeros((pad, LANES), F32)
    buf_ref[pad:pad + s, :] = lf
    d = 1
    while d < s:
        y = buf_ref[pad:pad + s, :] + buf_ref[pad - d:pad - d + s, :]
        buf_ref[pad:pad + s, :] = y
        d *= 2
    f = buf_ref[pad:pad + s, :]
    f_ref[0] = f
    for c in range(s // LANES):
        rows = slice(pad + c * LANES, pad + (c + 1) * LANES)
        t_ref[...] = buf_ref[rows, :].T
        ft_ref[0, :, c * LANES:(c + 1) * LANES] = t_ref[pl.ds(0, N_HEADS, stride=F_COPIES), :]
        t_ref[...] = lf_ref[0, c * LANES:(c + 1) * LANES, :].T
        lft_ref[0, :, c * LANES:(c + 1) * LANES] = t_ref[pl.ds(0, N_HEADS, stride=F_COPIES), :]


def _cumsum(lf):
    b, s, _ = lf.shape
    spec = pl.BlockSpec((1, s, LANES), lambda i: (i, 0, 0))
    spec_t = pl.BlockSpec((1, N_HEADS, s), lambda i: (i, 0, 0))
    shape_t = jax.ShapeDtypeStruct((b, N_HEADS, s), F32)
    return pl.pallas_call(
        _cumsum_kernel,
        grid=(b,),
        in_specs=[spec],
        out_specs=[spec, spec_t, spec_t],
        out_shape=[jax.ShapeDtypeStruct(lf.shape, F32), shape_t, shape_t],
        scratch_shapes=[pltpu.VMEM((s + s // 2, LANES), F32), pltpu.VMEM((LANES, LANES), F32)],
        compiler_params=pltpu.CompilerParams(dimension_semantics=("parallel",)),
        name="cumsum",
    )(lf)


def _aug_q(f, base, lane):
    hi, mid, lo = _split3(f)
    rel = lane - base
    one = jnp.where((rel >= 3) & (rel < 6), 1.0, 0.0)
    return jnp.where(rel == 0, hi, jnp.where(rel == 1, mid, jnp.where(rel == 2, lo, one))).astype(BF16)


def _aug_k_rows(f_row, first):
    hi, mid, lo = _split3(f_row)
    rel = lax.broadcasted_iota(jnp.int32, (BF16_ROWS, 1), 0) - first
    one = jnp.where((rel >= 0) & (rel < 3), 1.0, 0.0)
    return jnp.where(rel == 3, -hi, jnp.where(rel == 4, -mid, jnp.where(rel == 5, -lo, one))).astype(BF16)


def _attn_kernel(q_ref, k_ref, v_ref, fq_ref, fk_ref, o_ref, qcat_ref, kcat_ref, vcat_ref, *, t):
    hp = pl.program_id(1)
    n_t = kcat_ref.shape[1]
    lane = lax.broadcasted_iota(jnp.int32, (1, LANES), 1)
    f_base = hp * (HEADS_PER_BLOCK * F_COPIES)
    aug_first = pl.multiple_of(LANES + f_base, BF16_ROWS)
    dim_row = lax.broadcasted_iota(jnp.int32, (LANES, 1), 0)
    for j in range(n_t):
        cols = slice(j * t, (j + 1) * t)
        k2 = k_ref[0, :, cols].astype(BF16)
        v2 = v_ref[0, :, cols]
        q2 = q_ref[cols, :]
        fq = fq_ref[cols, :] * LOG2E
        for hh in range(HEADS_PER_BLOCK):
            kcat_ref[hh, j, 0:LANES, :] = k2
            kcat_ref[hh, j, LANES:2 * LANES, :] = jnp.zeros((LANES, t), BF16)
            fk = fk_ref[0, pl.ds(hp * HEADS_PER_BLOCK + hh, 1), cols] * LOG2E
            kcat_ref[hh, j, pl.ds(aug_first, BF16_ROWS), :] = _aug_k_rows(fk, hh * F_COPIES)
            own = (dim_row >= hh * HEAD_DIM) & (dim_row < (hh + 1) * HEAD_DIM)
            vcat_ref[hh, j] = jnp.where(own, v2, 1.0).astype(BF16)
            in_head = (lane >= hh * HEAD_DIM) & (lane < (hh + 1) * HEAD_DIM)
            qcat_ref[hh, j, :, 0:LANES] = jnp.where(in_head, q2, jnp.zeros_like(q2))
            qcat_ref[hh, j, :, LANES:2 * LANES] = _aug_q(fq, f_base + hh * F_COPIES, lane)

    causal = lax.broadcasted_iota(jnp.int32, (t, t), 1) <= lax.broadcasted_iota(jnp.int32, (t, t), 0)
    for i in range(n_t):
        outs = []
        for hh in range(HEADS_PER_BLOCK):
            m = acc = None
            for j in range(i + 1):
                s = _dot(qcat_ref[hh, i], kcat_ref[hh, j])
                if j == i:
                    s = jnp.where(causal, s, NEG_INF)
                m_tile = jnp.max(s, axis=1, keepdims=True)
                m_new = m_tile if j == 0 else jnp.maximum(m, m_tile)
                p = jnp.exp2(s - m_new).astype(BF16)
                pv = _dot_nt(p, vcat_ref[hh, j])
                acc = pv if j == 0 else jnp.exp2(m - m_new) * acc + pv
                m = m_new
            outs.append(acc / pltpu.roll(acc, HEAD_DIM, axis=1))
        o_ref[i * t:(i + 1) * t, :] = jnp.where(lane < HEAD_DIM, outs[0], outs[1]).astype(o_ref.dtype)


def _attn(q, kt, vt, f, ft, *, t):
    n, d = q.shape
    batch, _, s = kt.shape
    n_t = s // t
    n_hp = d // LANES
    q_spec = pl.BlockSpec((s, LANES), lambda b, hp: (b, hp))
    kv_spec = pl.BlockSpec((1, LANES, s), lambda b, hp: (b, hp, 0))
    return pl.pallas_call(
        functools.partial(_attn_kernel, t=t),
        grid=(batch, n_hp),
        in_specs=[q_spec, kv_spec, kv_spec,
                  pl.BlockSpec((s, LANES), lambda b, hp: (b, 0)),
                  pl.BlockSpec((1, N_HEADS, s), lambda b, hp: (b, 0, 0))],
        out_specs=q_spec,
        out_shape=jax.ShapeDtypeStruct((n, d), BF16),
        scratch_shapes=[pltpu.VMEM((HEADS_PER_BLOCK, n_t, t, 2 * LANES), BF16),
                        pltpu.VMEM((HEADS_PER_BLOCK, n_t, 2 * LANES, t), BF16),
                        pltpu.VMEM((HEADS_PER_BLOCK, n_t, LANES, t), BF16)],
        compiler_params=pltpu.CompilerParams(
            dimension_semantics=("parallel", "parallel"), vmem_limit_bytes=VMEM_LIMIT),
        name="attn",
    )(q, kt, vt, f, ft)


def _decode_kernel(pt_ref, q_ref, kn_ref, vn_ref, lfn_ref, *rest, n_p):
    k_refs = rest[0:n_p]
    v_refs = rest[n_p:2 * n_p]
    lf_refs = rest[2 * n_p:3 * n_p]
    o_ref = rest[3 * n_p]
    qrow_ref, m_ref, l_ref, acc_ref, fqcol_ref, carry_ref = rest[3 * n_p + 1:]
    del pt_ref
    i = pl.program_id(1)
    t_new, d = q_ref.shape
    rows = N_HEADS * t_new
    n_grp = d // (2 * LANES)
    grp_rows = rows // n_grp
    lane = lax.broadcasted_iota(jnp.int32, (1, LANES), 1)
    row = lax.broadcasted_iota(jnp.int32, (rows, 1), 0)

    @pl.when(i == 0)
    def _():
        qt = jnp.concatenate([q_ref[...]] * N_HEADS, axis=0)
        lane_head = lax.broadcasted_iota(jnp.int32, (1, d), 1) // HEAD_DIM
        qrow = jnp.where(lane_head == row // t_new, qt, 0.0).astype(BF16)
        qrow_ref[...] = qrow
        fq = lfn_ref[...]
        sub = lax.broadcasted_iota(jnp.int32, (t_new, 1), 0)
        sh = 1
        while sh < t_new:
            fq = fq + jnp.where(sub >= sh, pltpu.roll(fq, sh, axis=0), 0.0)
            sh *= 2
        gt = jnp.concatenate([fq, jnp.zeros((LANES - t_new, LANES), F32)], axis=0).T
        t_of_row = row % t_new
        fqcol = jnp.sum(jnp.where(lane == t_of_row, gt, 0.0), axis=1, keepdims=True)
        fqcol_ref[...] = fqcol
        pad = jnp.zeros((LANES - t_new, d), F32)
        kn = jnp.concatenate([kn_ref[...], pad], axis=0).astype(BF16)
        vn = jnp.concatenate([vn_ref[...], pad], axis=0).astype(BF16)
        s = _dot_nt(qrow, kn) + fqcol - gt
        s = jnp.where(lane <= t_of_row, s, NEG_INF)
        m = jnp.max(s, axis=1, keepdims=True)
        p = jnp.exp(s - m)
        m_ref[...] = m
        l_ref[...] = jnp.sum(p, axis=1, keepdims=True)
        pb = p.astype(BF16)
        for g in range(n_grp):
            r0 = g * grp_rows
            acc_ref[r0:r0 + grp_rows, :] = _dot(pb[r0:r0 + grp_rows, :], vn[:, g * 2 * LANES:(g + 1) * 2 * LANES])
        carry_ref[...] = jnp.zeros(carry_ref.shape, F32)

    carry = carry_ref[...]
    biases = []
    for p_i in range(n_p):
        x = lf_refs[p_i][0]
        incl = x
        sh = 1
        while sh < PAGE_SIZE:
            incl = incl + jnp.where(lane < PAGE_SIZE - sh, pltpu.roll(incl, PAGE_SIZE - sh, axis=1), 0.0)
            sh *= 2
        r = incl - x + carry
        carry = carry + incl[:, 0:1]
        biases.append(jnp.broadcast_to(r[:, None, :], (N_HEADS, t_new, PAGE_SIZE)).reshape(rows, PAGE_SIZE))
    carry_ref[...] = carry

    ktcat = jnp.concatenate([k_refs[p_i][0].astype(BF16) for p_i in range(n_p)], axis=1)
    s = _dot(qrow_ref[...], ktcat) + jnp.concatenate(biases, axis=1) + fqcol_ref[...]
    m_prev = m_ref[...]
    m_new = jnp.maximum(m_prev, jnp.max(s, axis=1, keepdims=True))
    alpha = jnp.exp(m_prev - m_new)
    p = jnp.exp(s - m_new)
    m_ref[...] = m_new
    l_ref[...] = alpha * l_ref[...] + jnp.sum(p, axis=1, keepdims=True)
    pb = p.astype(BF16)
    for g in range(n_grp):
        r0 = g * grp_rows
        c0 = g * 2 * LANES
        vt = jnp.concatenate([v_refs[p_i][0, c0:c0 + 2 * LANES, :].astype(BF16) for p_i in range(n_p)], axis=1)
        acc_ref[r0:r0 + grp_rows, :] = (alpha[r0:r0 + grp_rows] * acc_ref[r0:r0 + grp_rows, :]
                                        + _dot_nt(pb[r0:r0 + grp_rows, :], vt))

    @pl.when(i == pl.num_programs(1) - 1)
    def _():
        o = acc_ref[...] / l_ref[...]
        lane2 = lax.broadcasted_iota(jnp.int32, (1, 2 * LANES), 1)
        heads_per_grp = grp_rows // t_new
        for g in range(n_grp):
            out = o[g * grp_rows:g * grp_rows + t_new, :]
            for hh in range(1, heads_per_grp):
                r0 = g * grp_rows + hh * t_new
                out = jnp.where(lane2 >= hh * HEAD_DIM, o[r0:r0 + t_new, :], out)
            o_ref[:, g * 2 * LANES:(g + 1) * 2 * LANES] = out


def _decode(page_table, q, kn, vn, lfn, cache_kt, cache_vt, cache_lft, *, n_p):
    db, n_pages = page_table.shape
    n, d = q.shape
    t_new = n // db
    rows = N_HEADS * t_new
    pt_flat = page_table.reshape(-1)

    def page_map(p_i):
        def index_map(b, i, pt):
            return (pt[b * n_pages + n_pages - 1 - (i * n_p + p_i)], 0, 0)
        return index_map

    tok = lambda w: pl.BlockSpec((t_new, w), lambda b, i, pt: (b, 0))
    k_specs = [pl.BlockSpec((1, d, PAGE_SIZE), page_map(p_i)) for p_i in range(n_p)]
    lf_specs = [pl.BlockSpec((1, N_HEADS, PAGE_SIZE), page_map(p_i)) for p_i in range(n_p)]
    grid_spec = pltpu.PrefetchScalarGridSpec(
        num_scalar_prefetch=1,
        grid=(db, n_pages // n_p),
        in_specs=[tok(d), tok(d), tok(d), tok(LANES)] + k_specs + k_specs + lf_specs,
        out_specs=tok(d),
        scratch_shapes=[pltpu.VMEM((rows, d), BF16), pltpu.VMEM((rows, 1), F32), pltpu.VMEM((rows, 1), F32),
                        pltpu.VMEM((rows, 2 * LANES), F32), pltpu.VMEM((rows, 1), F32),
                        pltpu.VMEM((N_HEADS, 1), F32)],
    )
    return pl.pallas_call(
        functools.partial(_decode_kernel, n_p=n_p),
        grid_spec=grid_spec,
        out_shape=jax.ShapeDtypeStruct((n, d), F32),
        compiler_params=pltpu.CompilerParams(dimension_semantics=("parallel", "arbitrary"),
                                             vmem_limit_bytes=VMEM_LIMIT),
        name="decode",
    )(pt_flat, q, kn, vn, lfn, *([cache_kt] * n_p), *([cache_vt] * n_p), *([cache_lft] * n_p))


def _window_mean_minus_cur(load, cnt_of, out_dtype):
    outs = []
    for g, w in enumerate(POOL_WINDOWS):
        lo = g * POOL_GROUP_IN
        cur = load(0, lo)
        win = cur
        for j in range(1, w):
            win = win + load(j, lo)
        outs.append((win / cnt_of(w) - cur).astype(out_dtype))
    return outs


def _pool_prompt_kernel(u_ref, o_ref, buf_ref, *, chunk):
    s = u_ref.shape[1]
    halo = buf_ref.shape[0] - s
    buf_ref[0:halo, :] = jnp.zeros((halo, buf_ref.shape[1]), F32)
    buf_ref[halo:halo + s, :] = u_ref[0]
    for c in range(s // chunk):
        r0 = c * chunk
        pos = (lax.broadcasted_iota(jnp.int32, (chunk, 1), 0) + r0).astype(F32)
        load = lambda j, lo, r0=r0: buf_ref[halo + r0 - j:halo + r0 - j + chunk, lo:lo + POOL_GROUP_IN]
        cnt_of = lambda w, pos=pos: jnp.minimum(pos + 1.0, float(w))
        for g, slab in enumerate(_window_mean_minus_cur(load, cnt_of, o_ref.dtype)):
            o_ref[0, r0:r0 + chunk, g * POOL_GROUP_IN:(g + 1) * POOL_GROUP_IN] = slab


def _pool_prompt(u, *, chunk):
    b, s, dp = u.shape
    spec = pl.BlockSpec((1, s, dp), lambda i: (i, 0, 0))
    return pl.pallas_call(
        functools.partial(_pool_prompt_kernel, chunk=chunk),
        grid=(b,),
        in_specs=[spec],
        out_specs=spec,
        out_shape=jax.ShapeDtypeStruct(u.shape, BF16),
        scratch_shapes=[pltpu.VMEM((s + 2 * SUBLANES, dp), F32)],
        compiler_params=pltpu.CompilerParams(dimension_semantics=("parallel",), vmem_limit_bytes=VMEM_LIMIT),
        name="pool_prompt",
    )(u)


def _pool_sample_kernel(u_ref, o_ref, *, past_len):
    db, ext, _ = u_ref.shape
    t_new = o_ref.shape[0] // db
    first = ext - t_new
    pos = (lax.broadcasted_iota(jnp.int32, (1, t_new, 1), 1) + past_len).astype(F32)
    load = lambda j, lo: u_ref[:, first - j:first - j + t_new, lo:lo + POOL_GROUP_IN]
    cnt_of = lambda w: jnp.minimum(pos + 1.0, float(w))
    for g, slab in enumerate(_window_mean_minus_cur(load, cnt_of, F32)):
        o_ref[:, g * POOL_GROUP_IN:(g + 1) * POOL_GROUP_IN] = slab.reshape(db * t_new, POOL_GROUP_IN).astype(o_ref.dtype)


def _pool_sample(u_ext, *, t_new, past_len):
    db, ext, dp = u_ext.shape
    return pl.pallas_call(
        functools.partial(_pool_sample_kernel, past_len=past_len),
        grid=(1,),
        in_specs=[_const_spec(u_ext.shape)],
        out_specs=_const_spec((db * t_new, dp)),
        out_shape=jax.ShapeDtypeStruct((db * t_new, dp), BF16),
        compiler_params=pltpu.CompilerParams(vmem_limit_bytes=VMEM_LIMIT),
        name="pool_sample",
    )(u_ext)


def _post_kernel(a_ref, p_ref, sga_ref, sgb_ref, x_ref, wpool_ref, ps_ref, wout_ref, gpost_ref,
                 gmlp_ref, wup_ref, wdown_ref, gmlp2_ref, y_ref):
    pooled = p_ref[...]
    parts = [_dot(pooled[:, g * POOL_GROUP_IN:(g + 1) * POOL_GROUP_IN], wpool_ref[g])
             for g in range(len(POOL_WINDOWS))]
    py = jnp.concatenate(parts, axis=1) * ps_ref[...]
    z = sga_ref[...].astype(F32) * a_ref[...].astype(F32) + sgb_ref[...].astype(F32) * py
    x1 = x_ref[...] + _rms(_dot(z.astype(BF16), wout_ref[...]), gpost_ref[...])
    h2 = _rms(x1, gmlp_ref[...]).astype(BF16)
    r = jnp.maximum(_dot(h2, wup_ref[...]), 0.0)
    dn = _dot((r * r).astype(BF16), wdown_ref[...])
    y_ref[...] = x1 + _rms(dn, gmlp2_ref[...])


def _post(a, pooled, sga, sgb, x, wpool, ps, wout, gpost, gmlp, wup, wdown, gmlp2, *, tm):
    n, d = x.shape
    row = lambda w: pl.BlockSpec((tm, w), lambda i: (i, 0))
    consts = (wpool, ps, wout, gpost, gmlp, wup, wdown, gmlp2)
    return pl.pallas_call(
        _post_kernel,
        grid=(n // tm,),
        in_specs=[row(d), row(pooled.shape[1]), row(d), row(d), row(d)] + [_const_spec(c.shape) for c in consts],
        out_specs=row(d),
        out_shape=jax.ShapeDtypeStruct((n, d), F32),
        compiler_params=pltpu.CompilerParams(dimension_semantics=("parallel",), vmem_limit_bytes=VMEM_LIMIT),
        name="post",
    )(a, pooled, sga, sgb, x, *consts)


def _pick(n, pref):
    t = min(n, pref)
    assert n % t == 0, (n, t)
    return t


def _head_major(x):
    b, hd, s = x.shape
    if hd == N_HEADS:
        return jnp.transpose(x, (0, 2, 1))
    return jnp.transpose(x.reshape(b, N_HEADS, HEAD_DIM, s), (0, 3, 1, 2))


def kernel(x_prompt, x_sample, cache_k, cache_v, cache_logf, state_pool, page_table, norm_attn_pre, norm_attn_post,
           w_in, b_forget, w_pool, pool_scale, w_out, norm_mlp_pre, norm_mlp_post, w_up, w_down):
    b, s, d = x_prompt.shape
    db, t_new, _ = x_sample.shape
    depth = w_in.shape[0]
    n_pages = page_table.shape[1]
    n_phys = cache_k.shape[1]
    d_pool = state_pool.shape[-1]
    d_attn = N_HEADS * HEAD_DIM
    assert d == d_attn and d_pool == len(POOL_WINDOWS) * POOL_GROUP_IN and cache_k.shape[2] == PAGE_SIZE
    assert s % 256 == 0 and t_new == SUBLANES and N_HEADS * t_new == LANES
    n_p = max(p for p in (16, 8, 4, 2, 1) if n_pages % p == 0)
    c_f = 3 * d_attn
    c_u = c_f + N_HEADS
    c_g = c_u + d_pool

    xp = x_prompt.reshape(b * s, d)
    xs = x_sample.reshape(db * t_new, d)
    row = lambda v: v.reshape(1, -1)
    outs = [[] for _ in range(8)]
    for dd in range(depth):
        w = w_in[dd]
        wq = w[:, :d_attn].astype(BF16)
        wk = w[:, d_attn:2 * d_attn].astype(BF16)
        wv = w[:, 2 * d_attn:c_f].astype(BF16)
        wf = jnp.repeat(w[:, c_f:c_u], F_COPIES, axis=1).astype(BF16)
        bf = row(jnp.repeat(b_forget[dd], F_COPIES))
        wu = w[:, c_u:c_g].astype(BF16)
        wg = w[:, c_g:].astype(BF16)
        post_w = (w_pool[dd].astype(BF16), row(pool_scale[dd]), w_out[dd].astype(BF16), row(norm_attn_post[dd]),
                  row(norm_mlp_pre[dd]), w_up[dd].astype(BF16), w_down[dd].astype(BF16), row(norm_mlp_post[dd]))
        g_pre = row(norm_attn_pre[dd])

        q, kt, vt, lf, u, sga, sgb = _proj(xp, g_pre, wq, wk.T, wv.T, wf, bf, wu, wg, tm=_pick(s, 256),
                                           q_scale=SCALE * LOG2E, seq_len=s)
        f, ft, lft = _cumsum(lf.reshape(b, s, LANES))
        a = _attn(q, kt, vt, f.reshape(b * s, LANES), ft, t=256)
        pooled = _pool_prompt(u.reshape(b, s, d_pool), chunk=256).reshape(b * s, d_pool)
        xp = _post(a, pooled, sga, sgb, xp, *post_w, tm=_pick(b * s, 256))
        outs[0].append(_head_major(kt))
        outs[1].append(_head_major(vt))
        outs[2].append(_head_major(lft))
        outs[3].append(u.reshape(b, s, d_pool)[:, s - POOL_STATE:])

        q, k, v, lf, u, sga, sgb = _proj(xs, g_pre, wq, wk, wv, wf, bf, wu, wg, tm=_pick(db * t_new, 256),
                                         q_scale=SCALE)
        ckt = jnp.transpose(cache_k[dd], (0, 2, 3, 1)).reshape(n_phys, d_attn, PAGE_SIZE)
        cvt = jnp.transpose(cache_v[dd], (0, 2, 3, 1)).reshape(n_phys, d_attn, PAGE_SIZE)
        clt = jnp.swapaxes(cache_logf[dd], 1, 2)
        a = _decode(page_table, q.astype(F32), k, v, lf, ckt, cvt, clt, n_p=n_p)
        u_ext = jnp.concatenate([jnp.zeros((db, 1, d_pool), F32), state_pool[dd], u.reshape(db, t_new, d_pool)], axis=1)
        pooled = _pool_sample(u_ext, t_new=t_new, past_len=n_pages * PAGE_SIZE)
        xs = _post(a, pooled, sga, sgb, xs, *post_w, tm=_pick(db * t_new, 256))
        outs[4].append(k.reshape(db, t_new, N_HEADS, HEAD_DIM))
        outs[5].append(v.reshape(db, t_new, N_HEADS, HEAD_DIM))
        outs[6].append(lf[:, ::F_COPIES].reshape(db, t_new, N_HEADS))
        outs[7].append(u_ext[:, -POOL_STATE:])

    stacked = [jnp.stack(o) for o in outs]
    return (xp.reshape(b, s, d), xs.reshape(db, t_new, d), *stacked)
```

```python
import functools
import math

import jax
import jax.numpy as jnp
from jax import lax
from jax.experimental import pallas as pl
from jax.experimental.pallas import tpu as pltpu

N_HEADS = 16
HEAD_DIM = 64
POOL_WINDOWS = (2, 4, 8, 16)
POOL_STATE = max(POOL_WINDOWS) - 1
POOL_GROUP_IN = 128
POOL_GROUP_OUT = 256
RMS_EPS = 1e-6
PAGE_SIZE = 128
SCALE = 1.0 / math.sqrt(HEAD_DIM)
LOG2E = math.log2(math.e)

LANES = 128
SUBLANES = 8
BF16_ROWS = 16
HEADS_PER_BLOCK = LANES // HEAD_DIM
F_COPIES = LANES // N_HEADS
VMEM_LIMIT = 56 * 1024 * 1024

F32 = jnp.float32
BF16 = jnp.bfloat16
NEG_INF = float("-inf")

_dot = functools.partial(jnp.dot, preferred_element_type=F32)


def _dot_nt(a, b):
    return lax.dot_general(a, b, (((1,), (1,)), ((), ())), preferred_element_type=F32)


def _rms(x, g):
    return x * lax.rsqrt(jnp.mean(x * x, axis=-1, keepdims=True) + RMS_EPS) * g


def _sigmoid(x):
    return 1.0 / (1.0 + jnp.exp(-x))


def _log_sigmoid(x):
    return jnp.minimum(x, 0.0) - jnp.log1p(jnp.exp(-jnp.abs(x)))


def _split3(f):
    hi = f.astype(BF16).astype(F32)
    r = f - hi
    mid = r.astype(BF16).astype(F32)
    lo = (r - mid).astype(BF16).astype(F32)
    return hi, mid, lo


def _const_spec(shape):
    return pl.BlockSpec(shape, lambda *_: (0,) * len(shape))


def _proj_kernel(x_ref, g_ref, wq_ref, wk_ref, wv_ref, wf_ref, bf_ref, wu_ref, wg_ref,
                 q_ref, k_ref, v_ref, lf_ref, u_ref, sga_ref, sgb_ref, *, kv_transposed, q_scale):
    d = x_ref.shape[1]
    h = _rms(x_ref[...], g_ref[...]).astype(BF16)
    q_ref[...] = (_dot(h, wq_ref[...]) * q_scale).astype(BF16)
    if kv_transposed:
        k_ref[0] = _dot_nt(wk_ref[...], h)
        v_ref[0] = _dot_nt(wv_ref[...], h)
    else:
        k_ref[...] = _dot(h, wk_ref[...])
        v_ref[...] = _dot(h, wv_ref[...])
    lf_ref[...] = _log_sigmoid(_dot(h, wf_ref[...]) + bf_ref[...])
    u_ref[...] = _dot(h, wu_ref[...])
    sga_ref[...] = _sigmoid(_dot(h, wg_ref[:, 0:d])).astype(BF16)
    sgb_ref[...] = _sigmoid(_dot(h, wg_ref[:, d:2 * d])).astype(BF16)


def _proj(x, g, wq, wk, wv, wf, bf, wu, wg, *, tm, q_scale, seq_len=None):
    n, d = x.shape
    d_pool = wu.shape[1]
    row = lambda w: pl.BlockSpec((tm, w), lambda i: (i, 0))
    if seq_len is None:
        kv_spec = row(d)
        kv_shape = jax.ShapeDtypeStruct((n, d), F32)
    else:
        per_seq = seq_len // tm
        kv_spec = pl.BlockSpec((1, d, tm), lambda i: (i // per_seq, 0, i % per_seq))
        kv_shape = jax.ShapeDtypeStruct((n // seq_len, d, seq_len), F32)
    consts = (g, wq, wk, wv, wf, bf, wu, wg)
    return pl.pallas_call(
        functools.partial(_proj_kernel, kv_transposed=seq_len is not None, q_scale=q_scale),
        grid=(n // tm,),
        in_specs=[row(d)] + [_const_spec(c.shape) for c in consts],
        out_specs=[row(d), kv_spec, kv_spec, row(LANES), row(d_pool), row(d), row(d)],
        out_shape=[jax.ShapeDtypeStruct((n, d), BF16), kv_shape, kv_shape,
                   jax.ShapeDtypeStruct((n, LANES), F32), jax.ShapeDtypeStruct((n, d_pool), F32),
                   jax.ShapeDtypeStruct((n, d), BF16), jax.ShapeDtypeStruct((n, d), BF16)],
        compiler_params=pltpu.CompilerParams(dimension_semantics=("parallel",),
                                             vmem_limit_bytes=VMEM_LIMIT),
        name="proj",
    )(x, *consts)


def _cumsum_kernel(lf_ref, f_ref, ft_ref, lft_ref, buf_ref, t_ref):
    s = lf_ref.shape[1]
    pad = buf_ref.shape[0] - s
    lf = lf_ref[0]
    buf_ref[0:pad, :] = jnp.zeros((pad, LANES), F32)
    buf_ref[pad:pad + s, :] = lf
    d = 1
    while d < s:
        y = buf_ref[pad:pad + s, :] + buf_ref[pad - d:pad - d + s, :]
        buf_ref[pad:pad + s, :] = y
        d *= 2
    f = buf_ref[pad:pad + s, :]
    f_ref[0] = f
    for c in range(s // LANES):
        rows = slice(pad + c * LANES, pad + (c + 1) * LANES)
        t_ref[...] = buf_ref[rows, :].T
        ft_ref[0, :, c * LANES:(c + 1) * LANES] = t_ref[pl.ds(0, N_HEADS, stride=F_COPIES), :]
        t_ref[...] = lf_ref[0, c * LANES:(c + 1) * LANES, :].T
        lft_ref[0, :, c * LANES:(c + 1) * LANES] = t_ref[pl.ds(0, N_HEADS, stride=F_COPIES), :]


def _cumsum(lf):
    b, s, _ = lf.shape
    spec = pl.BlockSpec((1, s, LANES), lambda i: (i, 0, 0))
    spec_t = pl.BlockSpec((1, N_HEADS, s), lambda i: (i, 0, 0))
    shape_t = jax.ShapeDtypeStruct((b, N_HEADS, s), F32)
    return pl.pallas_call(
        _cumsum_kernel,
        grid=(b,),
        in_specs=[spec],
        out_specs=[spec, spec_t, spec_t],
        out_shape=[jax.ShapeDtypeStruct(lf.shape, F32), shape_t, shape_t],
        scratch_shapes=[pltpu.VMEM((s + s // 2, LANES), F32), pltpu.VMEM((LANES, LANES), F32)],
        compiler_params=pltpu.CompilerParams(dimension_semantics=("parallel",)),
        name="cumsum",
    )(lf)


def _aug_q(f, base, lane):
    hi, mid, lo = _split3(f)
    rel = lane - base
    one = jnp.where((rel >= 3) & (rel < 6), 1.0, 0.0)
    return jnp.where(rel == 0, hi, jnp.where(rel == 1, mid, jnp.where(rel == 2, lo, one))).astype(BF16)


def _aug_k_rows(f_row, first):
    hi, mid, lo = _split3(f_row)
    rel = lax.broadcasted_iota(jnp.int32, (BF16_ROWS, 1), 0) - first
    one = jnp.where((rel >= 0) & (rel < 3), 1.0, 0.0)
    return jnp.where(rel == 3, -hi, jnp.where(rel == 4, -mid, jnp.where(rel == 5, -lo, one))).astype(BF16)


def _attn_build(hp, q_ref, k_ref, v_ref, fq_ref, fk_ref, qcat_ref, kcat_ref, vcat_ref, t):
    n_t = kcat_ref.shape[1]
    lane = lax.broadcasted_iota(jnp.int32, (1, LANES), 1)
    f_base = hp * (HEADS_PER_BLOCK * F_COPIES)
    aug_first = pl.multiple_of(LANES + f_base, BF16_ROWS)
    dim_row = lax.broadcasted_iota(jnp.int32, (LANES, 1), 0)
    for j in range(n_t):
        cols = slice(j * t, (j + 1) * t)
        k2 = k_ref[0, :, cols].astype(BF16)
        v2 = v_ref[0, :, cols]
        q2 = q_ref[cols, :]
        fq = fq_ref[cols, :] * LOG2E
        for hh in range(HEADS_PER_BLOCK):
            kcat_ref[hh, j, 0:LANES, :] = k2
            kcat_ref[hh, j, LANES:2 * LANES, :] = jnp.zeros((LANES, t), BF16)
            fk = fk_ref[0, pl.ds(hp * HEADS_PER_BLOCK + hh, 1), cols] * LOG2E
            kcat_ref[hh, j, pl.ds(aug_first, BF16_ROWS), :] = _aug_k_rows(fk, hh * F_COPIES)
            own = (dim_row >= hh * HEAD_DIM) & (dim_row < (hh + 1) * HEAD_DIM)
            vcat_ref[hh, j] = jnp.where(own, v2, 1.0).astype(BF16)
            in_head = (lane >= hh * HEAD_DIM) & (lane < (hh + 1) * HEAD_DIM)
            qcat_ref[hh, j, :, 0:LANES] = jnp.where(in_head, q2, jnp.zeros_like(q2))
            qcat_ref[hh, j, :, LANES:2 * LANES] = _aug_q(fq, f_base + hh * F_COPIES, lane)


def _attn_tile(qcat, kcat, vcat, keep, m, acc, first):
    s = _dot(qcat, kcat)
    if keep is not None:
        s = jnp.where(keep, s, NEG_INF)
    m_tile = jnp.max(s, axis=1, keepdims=True)
    if first:
        p = jnp.exp2(s - m_tile).astype(BF16)
        return m_tile, _dot_nt(p, vcat)
    m_new = jnp.maximum(m, m_tile)
    p = jnp.exp2(s - m_new).astype(BF16)
    return m_new, jnp.exp2(m - m_new) * acc + _dot_nt(p, vcat)


def _attn_finish(accs, lane):
    outs = [acc / pltpu.roll(acc, HEAD_DIM, axis=1) for acc in accs]
    return jnp.where(lane < HEAD_DIM, outs[0], outs[1])


def _attn_kernel(q_ref, k_ref, v_ref, fq_ref, fk_ref, o_ref, qcat_ref, kcat_ref, vcat_ref, *, t):
    _attn_build(pl.program_id(1), q_ref, k_ref, v_ref, fq_ref, fk_ref, qcat_ref, kcat_ref, vcat_ref, t)
    n_t = kcat_ref.shape[1]
    lane = lax.broadcasted_iota(jnp.int32, (1, LANES), 1)
    dmat = lax.broadcasted_iota(jnp.int32, (2 * t, t), 1) - lax.broadcasted_iota(jnp.int32, (2 * t, t), 0)
    for i in range(0, n_t, 2):
        accs = []
        for hh in range(HEADS_PER_BLOCK):
            qcat = jnp.concatenate([qcat_ref[hh, i], qcat_ref[hh, i + 1]], axis=0)
            m = acc = None
            for j in range(i + 2):
                keep = dmat <= (i - j) * t if j >= i else None
                m, acc = _attn_tile(qcat, kcat_ref[hh, j], vcat_ref[hh, j], keep, m, acc, j == 0)
            accs.append(acc)
        o_ref[i * t:(i + 2) * t, :] = _attn_finish(accs, lane).astype(o_ref.dtype)


def _attn(q, kt, vt, f, ft, *, t):
    n, d = q.shape
    batch, _, s = kt.shape
    n_t = s // t
    n_hp = d // LANES
    q_spec = pl.BlockSpec((s, LANES), lambda b, hp: (b, hp))
    kv_spec = pl.BlockSpec((1, LANES, s), lambda b, hp: (b, hp, 0))
    return pl.pallas_call(
        functools.partial(_attn_kernel, t=t),
        grid=(batch, n_hp),
        in_specs=[q_spec, kv_spec, kv_spec,
                  pl.BlockSpec((s, LANES), lambda b, hp: (b, 0)),
                  pl.BlockSpec((1, N_HEADS, s), lambda b, hp: (b, 0, 0))],
        out_specs=q_spec,
        out_shape=jax.ShapeDtypeStruct((n, d), BF16),
        scratch_shapes=[pltpu.VMEM((HEADS_PER_BLOCK, n_t, t, 2 * LANES), BF16),
                        pltpu.VMEM((HEADS_PER_BLOCK, n_t, 2 * LANES, t), BF16),
                        pltpu.VMEM((HEADS_PER_BLOCK, n_t, LANES, t), BF16)],
        compiler_params=pltpu.CompilerParams(
            dimension_semantics=("parallel", "parallel"), vmem_limit_bytes=VMEM_LIMIT),
        name="attn",
    )(q, kt, vt, f, ft)


def _decode_scratch(t_new, d):
    rows = N_HEADS * t_new
    return [pltpu.VMEM((rows, d), BF16), pltpu.VMEM((rows, 1), F32), pltpu.VMEM((rows, 1), F32),
            pltpu.VMEM((rows, 2 * LANES), F32), pltpu.VMEM((rows, 1), F32), pltpu.VMEM((N_HEADS, 1), F32)]


def _decode_pages(i, q_ref, kn_ref, vn_ref, lfn_ref, k_refs, v_refs, lf_refs,
                  qrow_ref, m_ref, l_ref, acc_ref, fqcol_ref, carry_ref):
    n_p = len(k_refs)
    t_new, d = q_ref.shape
    rows = N_HEADS * t_new
    n_grp = d // (2 * LANES)
    grp_rows = rows // n_grp
    lane = lax.broadcasted_iota(jnp.int32, (1, LANES), 1)
    row = lax.broadcasted_iota(jnp.int32, (rows, 1), 0)

    @pl.when(i == 0)
    def _():
        qt = jnp.concatenate([q_ref[...]] * N_HEADS, axis=0)
        lane_head = lax.broadcasted_iota(jnp.int32, (1, d), 1) // HEAD_DIM
        qrow = jnp.where(lane_head == row // t_new, qt, 0.0).astype(BF16)
        qrow_ref[...] = qrow
        fq = lfn_ref[...]
        sub = lax.broadcasted_iota(jnp.int32, (t_new, 1), 0)
        sh = 1
        while sh < t_new:
            fq = fq + jnp.where(sub >= sh, pltpu.roll(fq, sh, axis=0), 0.0)
            sh *= 2
        gt = jnp.concatenate([fq, jnp.zeros((LANES - t_new, LANES), F32)], axis=0).T
        t_of_row = row % t_new
        fqcol = jnp.sum(jnp.where(lane == t_of_row, gt, 0.0), axis=1, keepdims=True)
        fqcol_ref[...] = fqcol
        pad = jnp.zeros((LANES - t_new, d), F32)
        kn = jnp.concatenate([kn_ref[...], pad], axis=0).astype(BF16)
        vn = jnp.concatenate([vn_ref[...], pad], axis=0).astype(BF16)
        s = _dot_nt(qrow, kn) + fqcol - gt
        s = jnp.where(lane <= t_of_row, s, NEG_INF)
        m = jnp.max(s, axis=1, keepdims=True)
        p = jnp.exp(s - m)
        m_ref[...] = m
        l_ref[...] = jnp.sum(p, axis=1, keepdims=True)
        pb = p.astype(BF16)
        for g in range(n_grp):
            r0 = g * grp_rows
            acc_ref[r0:r0 + grp_rows, :] = _dot(pb[r0:r0 + grp_rows, :], vn[:, g * 2 * LANES:(g + 1) * 2 * LANES])
        carry_ref[...] = jnp.zeros(carry_ref.shape, F32)

    carry = carry_ref[...]
    biases = []
    for p_i in range(n_p):
        x = lf_refs[p_i][0]
        incl = x
        sh = 1
        while sh < PAGE_SIZE:
            incl = incl + jnp.where(lane < PAGE_SIZE - sh, pltpu.roll(incl, PAGE_SIZE - sh, axis=1), 0.0)
            sh *= 2
        r = incl - x + carry
        carry = carry + incl[:, 0:1]
        biases.append(jnp.broadcast_to(r[:, None, :], (N_HEADS, t_new, PAGE_SIZE)).reshape(rows, PAGE_SIZE))
    carry_ref[...] = carry

    ktcat = jnp.concatenate([k_refs[p_i][0].astype(BF16) for p_i in range(n_p)], axis=1)
    s = _dot(qrow_ref[...], ktcat) + jnp.concatenate(biases, axis=1) + fqcol_ref[...]
    m_prev = m_ref[...]
    m_new = jnp.maximum(m_prev, jnp.max(s, axis=1, keepdims=True))
    alpha = jnp.exp(m_prev - m_new)
    p = jnp.exp(s - m_new)
    m_ref[...] = m_new
    l_ref[...] = alpha * l_ref[...] + jnp.sum(p, axis=1, keepdims=True)
    pb = p.astype(BF16)
    for g in range(n_grp):
        r0 = g * grp_rows
        c0 = g * 2 * LANES
        vt = jnp.concatenate([v_refs[p_i][0, c0:c0 + 2 * LANES, :].astype(BF16) for p_i in range(n_p)], axis=1)
        acc_ref[r0:r0 + grp_rows, :] = (alpha[r0:r0 + grp_rows] * acc_ref[r0:r0 + grp_rows, :]
                                        + _dot_nt(pb[r0:r0 + grp_rows, :], vt))


def _decode_finish(o_ref, l_ref, acc_ref):
    t_new, d = o_ref.shape
    n_grp = d // (2 * LANES)
    grp_rows = N_HEADS * t_new // n_grp
    o = acc_ref[...] / l_ref[...]
    lane2 = lax.broadcasted_iota(jnp.int32, (1, 2 * LANES), 1)
    for g in range(n_grp):
        out = o[g * grp_rows:g * grp_rows + t_new, :]
        for hh in range(1, grp_rows // t_new):
            r0 = g * grp_rows + hh * t_new
            out = jnp.where(lane2 >= hh * HEAD_DIM, o[r0:r0 + t_new, :], out)
        o_ref[:, g * 2 * LANES:(g + 1) * 2 * LANES] = out


def _decode_kernel(pt_ref, q_ref, kn_ref, vn_ref, lfn_ref, *rest, n_p):
    del pt_ref
    o_ref = rest[3 * n_p]
    scratch = rest[3 * n_p + 1:]
    i = pl.program_id(1)
    _decode_pages(i, q_ref, kn_ref, vn_ref, lfn_ref, rest[0:n_p], rest[n_p:2 * n_p], rest[2 * n_p:3 * n_p], *scratch)

    @pl.when(i == pl.num_programs(1) - 1)
    def _():
        _decode_finish(o_ref, scratch[2], scratch[3])


def _decode_specs(page_table, t_new, d, n_p):
    n_pages = page_table.shape[1]

    def page_map(p_i):
        def index_map(b, i, pt):
            return (pt[b * n_pages + n_pages - 1 - (i * n_p + p_i)], 0, 0)
        return index_map

    tok = lambda w: pl.BlockSpec((t_new, w), lambda b, i, pt: (b, 0))
    k_specs = [pl.BlockSpec((1, d, PAGE_SIZE), page_map(p_i)) for p_i in range(n_p)]
    lf_specs = [pl.BlockSpec((1, N_HEADS, PAGE_SIZE), page_map(p_i)) for p_i in range(n_p)]
    return [tok(d), tok(d), tok(d), tok(LANES)] + k_specs + k_specs + lf_specs, tok(d)


def _decode(page_table, q, kn, vn, lfn, cache_kt, cache_vt, cache_lft, *, n_p):
    db, n_pages = page_table.shape
    n, d = q.shape
    t_new = n // db
    in_specs, out_spec = _decode_specs(page_table, t_new, d, n_p)
    grid_spec = pltpu.PrefetchScalarGridSpec(
        num_scalar_prefetch=1,
        grid=(db, n_pages // n_p),
        in_specs=in_specs,
        out_specs=out_spec,
        scratch_shapes=_decode_scratch(t_new, d),
    )
    return pl.pallas_call(
        functools.partial(_decode_kernel, n_p=n_p),
        grid_spec=grid_spec,
        out_shape=jax.ShapeDtypeStruct((n, d), F32),
        compiler_params=pltpu.CompilerParams(dimension_semantics=("parallel", "arbitrary"),
                                             vmem_limit_bytes=VMEM_LIMIT),
        name="decode",
    )(page_table.reshape(-1), q, kn, vn, lfn, *([cache_kt] * n_p), *([cache_vt] * n_p), *([cache_lft] * n_p))


def _window_mean_minus_cur(load, cnt_of, out_dtype):
    outs = []
    for g, w in enumerate(POOL_WINDOWS):
        lo = g * POOL_GROUP_IN
        cur = load(0, lo)
        win = cur
        for j in range(1, w):
            win = win + load(j, lo)
        outs.append((win / cnt_of(w) - cur).astype(out_dtype))
    return outs


def _pool_prompt_kernel(u_ref, o_ref, buf_ref, *, chunk):
    s = u_ref.shape[1]
    halo = buf_ref.shape[0] - s
    buf_ref[0:halo, :] = jnp.zeros((halo, buf_ref.shape[1]), F32)
    buf_ref[halo:halo + s, :] = u_ref[0]
    for c in range(s // chunk):
        r0 = c * chunk
        pos = (lax.broadcasted_iota(jnp.int32, (chunk, 1), 0) + r0).astype(F32)
        load = lambda j, lo, r0=r0: buf_ref[halo + r0 - j:halo + r0 - j + chunk, lo:lo + POOL_GROUP_IN]
        cnt_of = lambda w, pos=pos: jnp.minimum(pos + 1.0, float(w))
        for g, slab in enumerate(_window_mean_minus_cur(load, cnt_of, o_ref.dtype)):
            o_ref[0, r0:r0 + chunk, g * POOL_GROUP_IN:(g + 1) * POOL_GROUP_IN] = slab


def _pool_prompt(u, *, chunk):
    b, s, dp = u.shape
    spec = pl.BlockSpec((1, s, dp), lambda i: (i, 0, 0))
    return pl.pallas_call(
        functools.partial(_pool_prompt_kernel, chunk=chunk),
        grid=(b,),
        in_specs=[spec],
        out_specs=spec,
        out_shape=jax.ShapeDtypeStruct(u.shape, BF16),
        scratch_shapes=[pltpu.VMEM((s + 2 * SUBLANES, dp), F32)],
        compiler_params=pltpu.CompilerParams(dimension_semantics=("parallel",), vmem_limit_bytes=VMEM_LIMIT),
        name="pool_prompt",
    )(u)


def _pool_sample_kernel(u_ref, o_ref, *, past_len):
    db, ext, _ = u_ref.shape
    t_new = o_ref.shape[0] // db
    first = ext - t_new
    pos = (lax.broadcasted_iota(jnp.int32, (1, t_new, 1), 1) + past_len).astype(F32)
    load = lambda j, lo: u_ref[:, first - j:first - j + t_new, lo:lo + POOL_GROUP_IN]
    cnt_of = lambda w: jnp.minimum(pos + 1.0, float(w))
    for g, slab in enumerate(_window_mean_minus_cur(load, cnt_of, F32)):
        o_ref[:, g * POOL_GROUP_IN:(g + 1) * POOL_GROUP_IN] = slab.reshape(db * t_new, POOL_GROUP_IN).astype(o_ref.dtype)


def _pool_sample(u_ext, *, t_new, past_len):
    db, ext, dp = u_ext.shape
    return pl.pallas_call(
        functools.partial(_pool_sample_kernel, past_len=past_len),
        grid=(1,),
        in_specs=[_const_spec(u_ext.shape)],
        out_specs=_const_spec((db * t_new, dp)),
        out_shape=jax.ShapeDtypeStruct((db * t_new, dp), BF16),
        compiler_params=pltpu.CompilerParams(vmem_limit_bytes=VMEM_LIMIT),
        name="pool_sample",
    )(u_ext)


def _post_kernel(a_ref, p_ref, sga_ref, sgb_ref, x_ref, wpool_ref, ps_ref, wout_ref, gpost_ref,
                 gmlp_ref, wup_ref, wdown_ref, gmlp2_ref, y_ref):
    pooled = p_ref[...]
    parts = [_dot(pooled[:, g * POOL_GROUP_IN:(g + 1) * POOL_GROUP_IN], wpool_ref[g])
             for g in range(len(POOL_WINDOWS))]
    py = jnp.concatenate(parts, axis=1) * ps_ref[...]
    z = sga_ref[...].astype(F32) * a_ref[...].astype(F32) + sgb_ref[...].astype(F32) * py
    x1 = x_ref[...] + _rms(_dot(z.astype(BF16), wout_ref[...]), gpost_ref[...])
    h2 = _rms(x1, gmlp_ref[...]).astype(BF16)
    r = jnp.maximum(_dot(h2, wup_ref[...]), 0.0)
    dn = _dot((r * r).astype(BF16), wdown_ref[...])
    y_ref[...] = x1 + _rms(dn, gmlp2_ref[...])


def _post(a, pooled, sga, sgb, x, wpool, ps, wout, gpost, gmlp, wup, wdown, gmlp2, *, tm):
    n, d = x.shape
    row = lambda w: pl.BlockSpec((tm, w), lambda i: (i, 0))
    consts = (wpool, ps, wout, gpost, gmlp, wup, wdown, gmlp2)
    return pl.pallas_call(
        _post_kernel,
        grid=(n // tm,),
        in_specs=[row(d), row(pooled.shape[1]), row(d), row(d), row(d)] + [_const_spec(c.shape) for c in consts],
        out_specs=row(d),
        out_shape=jax.ShapeDtypeStruct((n, d), F32),
        compiler_params=pltpu.CompilerParams(dimension_semantics=("parallel",), vmem_limit_bytes=VMEM_LIMIT),
        name="post",
    )(a, pooled, sga, sgb, x, *consts)


def _pick(n, pref):
    t = min(n, pref)
    assert n % t == 0, (n, t)
    return t


def _head_major(x):
    b, hd, s = x.shape
    if hd == N_HEADS:
        return jnp.transpose(x, (0, 2, 1))
    return jnp.transpose(x.reshape(b, N_HEADS, HEAD_DIM, s), (0, 3, 1, 2))


def kernel(x_prompt, x_sample, cache_k, cache_v, cache_logf, state_pool, page_table, norm_attn_pre, norm_attn_post,
           w_in, b_forget, w_pool, pool_scale, w_out, norm_mlp_pre, norm_mlp_post, w_up, w_down):
    b, s, d = x_prompt.shape
    db, t_new, _ = x_sample.shape
    depth = w_in.shape[0]
    n_pages = page_table.shape[1]
    n_phys = cache_k.shape[1]
    d_pool = state_pool.shape[-1]
    d_attn = N_HEADS * HEAD_DIM
    assert d == d_attn and d_pool == len(POOL_WINDOWS) * POOL_GROUP_IN and cache_k.shape[2] == PAGE_SIZE
    t_attn = 256
    assert s % (2 * t_attn) == 0 and t_new == SUBLANES and N_HEADS * t_new == LANES
    n_p = max(p for p in (16, 8, 4, 2, 1) if n_pages % p == 0)
    c_f = 3 * d_attn
    c_u = c_f + N_HEADS
    c_g = c_u + d_pool

    xp = x_prompt.reshape(b * s, d)
    xs = x_sample.reshape(db * t_new, d)
    row = lambda v: v.reshape(1, -1)
    outs = [[] for _ in range(8)]
    for dd in range(depth):
        w = w_in[dd]
        wq = w[:, :d_attn].astype(BF16)
        wk = w[:, d_attn:2 * d_attn].astype(BF16)
        wv = w[:, 2 * d_attn:c_f].astype(BF16)
        wf = jnp.repeat(w[:, c_f:c_u], F_COPIES, axis=1).astype(BF16)
        bf = row(jnp.repeat(b_forget[dd], F_COPIES))
        wu = w[:, c_u:c_g].astype(BF16)
        wg = w[:, c_g:].astype(BF16)
        post_w = (w_pool[dd].astype(BF16), row(pool_scale[dd]), w_out[dd].astype(BF16), row(norm_attn_post[dd]),
                  row(norm_mlp_pre[dd]), w_up[dd].astype(BF16), w_down[dd].astype(BF16), row(norm_mlp_post[dd]))
        g_pre = row(norm_attn_pre[dd])

        q, kt, vt, lf, u, sga, sgb = _proj(xp, g_pre, wq, wk.T, wv.T, wf, bf, wu, wg, tm=_pick(s, 256),
                                           q_scale=SCALE * LOG2E, seq_len=s)
        f, ft, lft = _cumsum(lf.reshape(b, s, LANES))
        attn_args = (q, kt, vt, f.reshape(b * s, LANES), ft)
        pooled_p = _pool_prompt(u.reshape(b, s, d_pool), chunk=256).reshape(b * s, d_pool)
        gates_p = (sga, sgb)
        outs[0].append(_head_major(kt))
        outs[1].append(_head_major(vt))
        outs[2].append(_head_major(lft))
        outs[3].append(u.reshape(b, s, d_pool)[:, s - POOL_STATE:])

        q, k, v, lf, u, sga, sgb = _proj(xs, g_pre, wq, wk, wv, wf, bf, wu, wg, tm=_pick(db * t_new, 256),
                                         q_scale=SCALE)
        ckt = jnp.transpose(cache_k[dd], (0, 2, 3, 1)).reshape(n_phys, d_attn, PAGE_SIZE)
        cvt = jnp.transpose(cache_v[dd], (0, 2, 3, 1)).reshape(n_phys, d_attn, PAGE_SIZE)
        clt = jnp.swapaxes(cache_logf[dd], 1, 2)
        a_p = _attn(*attn_args, t=t_attn)
        a = _decode(page_table, q.astype(F32), k, v, lf, ckt, cvt, clt, n_p=n_p)
        xp = _post(a_p, pooled_p, *gates_p, xp, *post_w, tm=_pick(b * s, 256))
        u_ext = jnp.concatenate([jnp.zeros((db, 1, d_pool), F32), state_pool[dd], u.reshape(db, t_new, d_pool)], axis=1)
        pooled = _pool_sample(u_ext, t_new=t_new, past_len=n_pages * PAGE_SIZE)
        xs = _post(a, pooled, sga, sgb, xs, *post_w, tm=_pick(db * t_new, 256))
        outs[4].append(k.reshape(db, t_new, N_HEADS, HEAD_DIM))
        outs[5].append(v.reshape(db, t_new, N_HEADS, HEAD_DIM))
        outs[6].append(lf[:, ::F_COPIES].reshape(db, t_new, N_HEADS))
        outs[7].append(u_ext[:, -POOL_STATE:])

    stacked = [jnp.stack(o) for o in outs]
    return (xp.reshape(b, s, d), xs.reshape(db, t_new, d), *stacked)
```
